```python
import jax, jax.numpy as jnp
from jax import lax
import numpy as np

D_MODEL = 1024
BATCH = 16
SEQ = 4096
DEPTH = 2
DEC_BATCH = 16
DEC_SEQ = 16
PAST_LEN = 4096

CHUNK = 64
Q_BLOCK = 128
MIX_DIM = D_MODEL
D_FF = 2816
FFN_RES = 0.5
EPS = 1e-6
MLA_HEADS = 8
MLA_NOPE = 64
MLA_ROPE = 32
MLA_V = 64
MLA_Q_LORA = 256
MLA_KV_LORA = 128
ROPE_THETA = 10000.0
MLA_SCALE = (MLA_NOPE + MLA_ROPE) ** -0.5
ML_HEADS = 4
ML_DK = 64
ML_DV = 64
ML_CONV = 4
HG_HEADS = 4
HG_DK = 64
HG_DV = 64

SPLIT_SIZES = (MLA_Q_LORA, MLA_KV_LORA, MLA_ROPE,
               2 * ML_HEADS * ML_DK, ML_HEADS * ML_DV, ML_HEADS * ML_DV, ML_HEADS, ML_HEADS,
               HG_HEADS * HG_DK, HG_HEADS * HG_DK, HG_HEADS * HG_DV, HG_HEADS * HG_DV)
IN_DIM = sum(SPLIT_SIZES)

kernel_name = 'hybrid_stream_mla_mlstm_hgrn2_step'


def _split_points():
    pts, acc = [], 0
    for s in SPLIT_SIZES[:-1]:
        acc += s
        pts.append(acc)
    return pts


def rmsnorm(x, g):
    xf = x.astype(jnp.float32)
    y = xf * lax.rsqrt(jnp.mean(xf * xf, axis=-1, keepdims=True) + EPS)
    return (y * g.astype(jnp.float32)).astype(x.dtype)


def swiglu(x, w_in, w_out):
    gate, up = jnp.split(x @ w_in, 2, axis=-1)
    return (jax.nn.silu(gate) * up) @ w_out


def rope_tables(pos):
    inv = ROPE_THETA ** (-jnp.arange(0, MLA_ROPE, 2, dtype=jnp.float32) / MLA_ROPE)
    ang = pos.astype(jnp.float32)[:, None] * inv[None, :]
    return jnp.cos(ang), jnp.sin(ang)


def apply_rope(x, cos, sin):
    x1, x2 = jnp.split(x.astype(jnp.float32), 2, axis=-1)
    return jnp.concatenate([x1 * cos - x2 * sin, x2 * cos + x1 * sin], axis=-1).astype(x.dtype)


def mla_attend(q_lat, q_pe, ckv, kpe, q_pos, k_pos):
    s = jnp.einsum('bqhc,bkc->bhqk', q_lat, ckv) + jnp.einsum('bqhr,bkr->bhqk', q_pe, kpe)
    s = s.astype(jnp.float32) * MLA_SCALE
    mask = (k_pos[None, :] // CHUNK) <= (q_pos[:, None] // CHUNK)
    s = jnp.where(mask[None, None], s, jnp.finfo(jnp.float32).min)
    p = jax.nn.softmax(s, axis=-1).astype(ckv.dtype)
    return jnp.einsum('bhqk,bkc->bqhc', p, ckv)


def mla_prompt(q_lat, q_pe, ckv, kpe, pos):
    B, T, H, C = q_lat.shape
    nb = T // Q_BLOCK

    def blocks(a):
        return jnp.moveaxis(a.reshape((B, nb, Q_BLOCK) + a.shape[2:]), 1, 0)

    def one(args):
        ql, qp, qpos = args
        return mla_attend(ql, qp, ckv, kpe, qpos, pos)

    o = lax.map(one, (blocks(q_lat), blocks(q_pe), pos.reshape(nb, Q_BLOCK)))
    return jnp.moveaxis(o, 0, 1).reshape(B, T, H, C)


def mlstm_chunkwise(q, k, v, li, lf, c0, n0, m0):
    B, T, H, _ = q.shape
    L = min(CHUNK, T)
    nc = T // L
    causal = jnp.tril(jnp.ones((L, L), dtype=bool))

    def chunks(a):
        return jnp.moveaxis(a.reshape((B, nc, L) + a.shape[2:]), 1, 0)

    def step(carry, inp):
        c, n, m = carry
        qc, kc, vc, lic, lfc = inp
        b = jnp.cumsum(lfc, axis=1)
        logw = b[:, :, None, :] - b[:, None, :, :] + lic[:, None, :, :]
        logw = jnp.where(causal[None, :, :, None], logw, -jnp.inf)
        log_inter = b + m[:, None, :]
        m_t = jnp.maximum(log_inter, jnp.max(logw, axis=2))
        w_intra = jnp.exp(logw - m_t[:, :, None, :])
        w_inter = jnp.exp(log_inter - m_t)
        qk = jnp.einsum('bthd,bshd->btsh', qc, kc) * w_intra
        num = jnp.einsum('btsh,bshv->bthv', qk, vc) + w_inter[..., None] * jnp.einsum('bthd,bhdv->bthv', qc, c)
        den = jnp.sum(qk, axis=2) + w_inter * jnp.einsum('bthd,bhd->bth', qc, n)
        h = num / jnp.maximum(jnp.abs(den), jnp.exp(-m_t))[..., None]
        m_new = m_t[:, -1]
        w_s = jnp.exp(b[:, -1:] - b + lic - m_new[:, None])
        decay = jnp.exp(b[:, -1] + m - m_new)
        kw = kc * w_s[..., None]
        c_new = decay[..., None, None] * c + jnp.einsum('bshd,bshv->bhdv', kw, vc)
        n_new = decay[..., None] * n + jnp.sum(kw, axis=1)
        return (c_new, n_new, m_new), h

    (c1, n1, m1), h = lax.scan(step, (c0, n0, m0), tuple(chunks(a) for a in (q, k, v, li, lf)))
    h = jnp.moveaxis(h, 0, 1).reshape(B, T, H, v.shape[-1])
    return h, c1, n1, m1


def hgrn2_chunkwise(q, k, i, logf, s0):
    B, T, H, _ = q.shape
    L = min(CHUNK, T)
    nc = T // L
    causal = jnp.tril(jnp.ones((L, L), dtype=bool))

    def chunks(a):
        return jnp.moveaxis(a.reshape((B, nc, L) + a.shape[2:]), 1, 0)

    def step(s, inp):
        qc, kc, ic, lfc = inp
        b = jnp.cumsum(lfc, axis=1)
        diff = b[:, :, None] - b[:, None]
        dec = jnp.exp(jnp.where(causal[None, :, :, None, None], diff, -jnp.inf))
        a = jnp.sum(qc[:, :, None] * kc[:, None] * dec, axis=-1)
        o = jnp.einsum('btsh,bshv->bthv', a, ic) + jnp.einsum('bthd,bhdv->bthv', qc * jnp.exp(b), s)
        b_last = b[:, -1]
        s_new = jnp.exp(b_last)[..., None] * s + jnp.einsum('bshd,bshv->bhdv', kc * jnp.exp(b_last[:, None] - b), ic)
        return s_new, o

    s1, o = lax.scan(step, s0, tuple(chunks(a) for a in (q, k, i, logf)))
    o = jnp.moveaxis(o, 0, 1).reshape(B, T, H, i.shape[-1])
    return o, s1


def token_mixers(hn, pos, past, lw):
    B, T, _ = hn.shape
    f32 = jnp.float32
    (cq, ckv_raw, kpe_raw, ml_qk, ml_v, ml_o, ml_i, ml_f,
     hg_q, hg_f, hg_i, hg_g) = jnp.split(hn @ lw['w_in'], _split_points(), axis=-1)

    cq = rmsnorm(cq, lw['q_norm'])
    q = (cq @ lw['w_uq']).reshape(B, T, MLA_HEADS, MLA_NOPE + MLA_ROPE)
    cos, sin = rope_tables(pos)
    q_pe = apply_rope(q[..., MLA_NOPE:], cos[:, None], sin[:, None])
    q_lat = jnp.einsum('bthn,chn->bthc', q[..., :MLA_NOPE], lw['w_uk'])
    ckv_new = rmsnorm(ckv_raw, lw['kv_norm'])
    kpe_new = apply_rope(kpe_raw, cos, sin)
    if past is None:
        o_lat = mla_prompt(q_lat, q_pe, ckv_new, kpe_new, pos)
    else:
        ckv_all = jnp.concatenate([past[0].astype(ckv_new.dtype), ckv_new], axis=1)
        kpe_all = jnp.concatenate([past[1].astype(kpe_new.dtype), kpe_new], axis=1)
        k_pos = jnp.arange(ckv_all.shape[1])
        o_lat = mla_attend(q_lat, q_pe, ckv_all, kpe_all, pos, k_pos)
    mla_out = jnp.einsum('bthc,chv->bthv', o_lat, lw['w_uv']).reshape(B, T, MLA_HEADS * MLA_V)

    if past is None:
        conv0 = jnp.zeros((B, ML_CONV - 1, 2 * ML_HEADS * ML_DK), ml_qk.dtype)
        c0 = jnp.zeros((B, ML_HEADS, ML_DK, ML_DV), f32)
        n0 = jnp.zeros((B, ML_HEADS, ML_DK), f32)
        m0 = jnp.zeros((B, ML_HEADS), f32)
        s0 = jnp.zeros((B, HG_HEADS, HG_DK, HG_DV), f32)
    else:
        conv0 = past[5].astype(ml_qk.dtype)
        c0, n0, m0, s0 = (past[2].astype(f32), past[3].astype(f32), past[4].astype(f32), past[6].astype(f32))
    ext = jnp.concatenate([conv0, ml_qk], axis=1)
    conv_new = ext[:, -(ML_CONV - 1):]
    nch = ext.shape[-1]
    conv = lax.conv_general_dilated(ext, lw['conv_w'].astype(ext.dtype)[:, None, :], window_strides=(1,),
                                    padding='VALID', dimension_numbers=('NWC', 'WIO', 'NWC'),
                                    feature_group_count=nch)
    qk = jax.nn.silu(conv + lw['conv_b'].astype(conv.dtype)).astype(f32)
    mq, mk = jnp.split(qk, 2, axis=-1)
    mq = mq.reshape(B, T, ML_HEADS, ML_DK)
    mk = mk.reshape(B, T, ML_HEADS, ML_DK) * (ML_DK ** -0.5)
    mv = ml_v.astype(f32).reshape(B, T, ML_HEADS, ML_DV)
    gb = lw['gate_bias'].astype(f32)
    li = ml_i.astype(f32) + gb[0]
    lf = jax.nn.log_sigmoid(ml_f.astype(f32) + gb[1])
    h, c1, n1, m1 = mlstm_chunkwise(mq, mk, mv, li, lf, c0, n0, m0)
    ml_out = (jax.nn.sigmoid(ml_o.astype(f32)) * h.reshape(B, T, ML_HEADS * ML_DV)).astype(hn.dtype)

    lb = lw['lb']
    z = hg_f.astype(f32)
    logf = jnp.logaddexp(jnp.log(lb), jnp.log1p(-lb) + jax.nn.log_sigmoid(z))
    kk = (1.0 - lb) * jax.nn.sigmoid(-z)
    hq = jax.nn.silu(hg_q.astype(f32))
    sh = (B, T, HG_HEADS, HG_DK)
    o, s1 = hgrn2_chunkwise(hq.reshape(sh), kk.reshape(sh), hg_i.astype(f32).reshape(B, T, HG_HEADS, HG_DV),
                            logf.reshape(sh), s0)
    gate = jax.nn.silu(hg_g.astype(f32).reshape(B, T, HG_HEADS, HG_DV))
    hg_out = (rmsnorm(o, lw['hg_norm']) * gate).reshape(B, T, HG_HEADS * HG_DV).astype(hn.dtype)

    mix = jnp.concatenate([mla_out, ml_out, hg_out], axis=-1) @ lw['w_out']
    return mix, (ckv_new, kpe_new, c1, n1, m1, conv_new, s1)


def trunk_layer(x, pos, past, lw):
    g = lw['ln']
    h = swiglu(rmsnorm(x, g[0]), lw['ffn_in'][0], lw['ffn_out'][0])
    x = x + FFN_RES * rmsnorm(h, g[1])
    mix, state = token_mixers(rmsnorm(x, g[2]), pos, past, lw)
    x = x + rmsnorm(mix, g[3])
    h = swiglu(rmsnorm(x, g[4]), lw['ffn_in'][1], lw['ffn_out'][1])
    x = x + FFN_RES * rmsnorm(h, g[5])
    return x, state


def setup_inputs(seed: int = 0) -> dict:
    key = jax.random.key(seed)
    ks = jax.random.split(key, 26)
    f32 = jnp.float32

    def nrm(k, shape, s):
        return s * jax.random.normal(k, shape, f32)

    f_bias = jnp.broadcast_to(jnp.linspace(3.0, 6.0, ML_HEADS, dtype=f32), (DEPTH, ML_HEADS)) + nrm(ks[24], (DEPTH, ML_HEADS), 0.1)
    gate_bias = jnp.stack([nrm(ks[21], (DEPTH, ML_HEADS), 0.1), f_bias], axis=1)
    return {
        'x_prompt': nrm(ks[0], (BATCH, SEQ, D_MODEL), 1.0),
        'x_sample': nrm(ks[1], (DEC_BATCH, DEC_SEQ, D_MODEL), 1.0),
        'cache_ckv': nrm(ks[2], (DEPTH, DEC_BATCH, PAST_LEN, MLA_KV_LORA), 1.0),
        'cache_kpe': nrm(ks[3], (DEPTH, DEC_BATCH, PAST_LEN, MLA_ROPE), 1.0),
        'state_mlstm_c': nrm(ks[4], (DEPTH, DEC_BATCH, ML_HEADS, ML_DK, ML_DV), 0.1),
        'state_mlstm_n': nrm(ks[5], (DEPTH, DEC_BATCH, ML_HEADS, ML_DK), 0.5),
        'state_mlstm_m': nrm(ks[6], (DEPTH, DEC_BATCH, ML_HEADS), 1.0),
        'state_mlstm_conv': nrm(ks[7], (DEPTH, DEC_BATCH, ML_CONV - 1, 2 * ML_HEADS * ML_DK), 1.0),
        'state_hgrn': nrm(ks[8], (DEPTH, DEC_BATCH, HG_HEADS, HG_DK, HG_DV), 0.3),
        'ln_gains': 1.0 + nrm(ks[9], (DEPTH, 6, D_MODEL), 0.05),
        'w_ffn_in': nrm(ks[10], (DEPTH, 2, D_MODEL, 2 * D_FF), D_MODEL ** -0.5),
        'w_ffn_out': nrm(ks[11], (DEPTH, 2, D_FF, D_MODEL), D_FF ** -0.5),
        'w_in': nrm(ks[12], (DEPTH, D_MODEL, IN_DIM), D_MODEL ** -0.5),
        'w_out': nrm(ks[13], (DEPTH, MIX_DIM, D_MODEL), MIX_DIM ** -0.5),
        'mla_q_norm': 1.0 + nrm(ks[14], (DEPTH, MLA_Q_LORA), 0.05),
        'mla_kv_norm': 1.0 + nrm(ks[15], (DEPTH, MLA_KV_LORA), 0.05),
        'mla_w_uq': nrm(ks[16], (DEPTH, MLA_Q_LORA, MLA_HEADS * (MLA_NOPE + MLA_ROPE)), MLA_Q_LORA ** -0.5),
        'mla_w_uk': nrm(ks[17], (DEPTH, MLA_KV_LORA, MLA_HEADS, MLA_NOPE), MLA_KV_LORA ** -0.5),
        'mla_w_uv': nrm(ks[18], (DEPTH, MLA_KV_LORA, MLA_HEADS, MLA_V), MLA_KV_LORA ** -0.5),
        'ml_conv_w': nrm(ks[19], (DEPTH, ML_CONV, 2 * ML_HEADS * ML_DK), ML_CONV ** -0.5),
        'ml_conv_b': nrm(ks[20], (DEPTH, 2 * ML_HEADS * ML_DK), 0.01),
        'ml_gate_bias': gate_bias,
        'hg_lb_raw': nrm(ks[22], (DEPTH, HG_HEADS * HG_DK), 0.5),
        'hg_norm': 1.0 + nrm(ks[23], (DEPTH, HG_DV), 0.05),
    }


def reference(x_prompt, x_sample, cache_ckv, cache_kpe, state_mlstm_c, state_mlstm_n, state_mlstm_m,
              state_mlstm_conv, state_hgrn, ln_gains, w_ffn_in, w_ffn_out, w_in, w_out,
              mla_q_norm, mla_kv_norm, mla_w_uq, mla_w_uk, mla_w_uv, ml_conv_w, ml_conv_b,
              ml_gate_bias, hg_lb_raw, hg_norm):
    lb_all = jnp.cumsum(jax.nn.softmax(hg_lb_raw.astype(jnp.float32), axis=0), axis=0)
    lb_all = lb_all - lb_all[:1]
    past_len = cache_ckv.shape[2]
    pos_p = jnp.arange(x_prompt.shape[1])
    pos_s = past_len + jnp.arange(x_sample.shape[1])
    yp, ys = x_prompt, x_sample
    p_lists = [[] for _ in range(7)]
    s_lists = [[] for _ in range(7)]
    for l in range(DEPTH):
        lw = {'ln': ln_gains[l], 'ffn_in': w_ffn_in[l], 'ffn_out': w_ffn_out[l], 'w_in': w_in[l],
              'w_out': w_out[l], 'q_norm': mla_q_norm[l], 'kv_norm': mla_kv_norm[l], 'w_uq': mla_w_uq[l],
              'w_uk': mla_w_uk[l], 'w_uv': mla_w_uv[l], 'conv_w': ml_conv_w[l], 'conv_b': ml_conv_b[l],
              'gate_bias': ml_gate_bias[l], 'lb': lb_all[l], 'hg_norm': hg_norm[l]}
        yp, st_p = trunk_layer(yp, pos_p, None, lw)
        past = (cache_ckv[l], cache_kpe[l], state_mlstm_c[l], state_mlstm_n[l], state_mlstm_m[l],
                state_mlstm_conv[l], state_hgrn[l])
        ys, st_s = trunk_layer(ys, pos_s, past, lw)
        for lst, a in zip(p_lists, st_p):
            lst.append(a)
        for lst, a in zip(s_lists, st_s):
            lst.append(a)
    p_ckv, p_kpe, p_c, p_n, p_m, p_conv, p_hg = [jnp.stack(a) for a in p_lists]
    s_ckv, s_kpe, s_c, s_n, s_m, s_conv, s_hg = [jnp.stack(a) for a in s_lists]
    return (yp, ys, p_ckv, p_kpe, p_c, p_n, p_m, p_conv, p_hg,
            s_ckv, s_kpe, s_c, s_n, s_m, s_conv, s_hg)
```

```python
import functools

import numpy as np
import jax
import jax.numpy as jnp
from jax import lax
from jax.experimental import pallas as pl
from jax.experimental.pallas import tpu as pltpu

F32 = jnp.float32
BF16 = jnp.bfloat16

CHUNK = 64
FFN_RES = 0.5
EPS = 1e-6
MLA_HEADS = 8
MLA_NOPE = 64
MLA_ROPE = 32
MLA_V = 64
MLA_Q_LORA = 256
MLA_KV_LORA = 128
ROPE_THETA = 10000.0
MLA_SCALE = (MLA_NOPE + MLA_ROPE) ** -0.5
ML_HEADS = 4
ML_DK = 64
ML_DV = 64
ML_CONV = 4
HG_HEADS = 4
HG_DK = 64
HG_DV = 64

LANES = 128
VMEM_LIMIT = 56 * 1024 * 1024
QW = 2 * LANES
HG_SUB = 16
NEG_BIG = -1e30


def _rms(x, g):
    return x * lax.rsqrt(jnp.mean(x * x, axis=-1, keepdims=True) + EPS) * g


def _dot(a, b):
    return jnp.dot(a, b, preferred_element_type=F32)


def _dot_nt(a, b):
    return lax.dot_general(a, b, (((1,), (1,)), ((), ())), preferred_element_type=F32)


def _dot_tn(a, b):
    return lax.dot_general(a, b, (((0,), (0,)), ((), ())), preferred_element_type=F32)


def _split3(x):
    hi = x.astype(BF16)
    r = x - hi.astype(F32)
    mid = r.astype(BF16)
    lo = (r - mid.astype(F32)).astype(BF16)
    return hi, mid, lo


def _sel_dot(sel, x):
    hi, mid, lo = _split3(x)
    return _dot(sel, hi) + _dot(sel, mid) + _dot(sel, lo)


def _sel_dot_nt(sel, x):
    hi, mid, lo = _split3(x)
    return _dot_nt(sel, hi) + _dot_nt(sel, mid) + _dot_nt(sel, lo)


def _lane_group(n, g):
    return lax.broadcasted_iota(jnp.int32, (1, n), 1) // g


def _expand_cols(cols, g):
    grp = _lane_group(len(cols) * g, g)
    out = cols[-1]
    for h in range(len(cols) - 2, -1, -1):
        out = jnp.where(grp == h, cols[h], out)
    return out


def _seg_sum(x, g, heads):
    grp = _lane_group(heads * g, g)
    return [jnp.sum(jnp.where(grp == h, x, 0.0), axis=-1, keepdims=True) for h in range(heads)]


def _seg_max(x, g, heads):
    grp = _lane_group(heads * g, g)
    return [jnp.max(jnp.where(grp == h, x, -jnp.inf), axis=-1, keepdims=True) for h in range(heads)]


def _const_spec(shape):
    nd = len(shape)
    return pl.BlockSpec(shape, lambda *_: (0,) * nd, pipeline_mode=pl.Buffered(1))


def _params(sem):
    return pltpu.CompilerParams(dimension_semantics=sem, vmem_limit_bytes=VMEM_LIMIT)


def _token_tile(n):
    for t in (512, 256, 128, 64, 32, 16, 8):
        if n % t == 0:
            return t
    raise ValueError(f"token count {n} must be a multiple of 8")


def _bmm_body(a_ref, b_ref, o_ref):
    o_ref[0] = jnp.dot(a_ref[0], b_ref[0], preferred_element_type=F32, precision=lax.Precision.HIGHEST)


def _bmm(a, b):
    h, m, k = a.shape
    n = b.shape[2]
    return pl.pallas_call(
        _bmm_body,
        grid=(h,),
        in_specs=[pl.BlockSpec((1, m, k), lambda i: (i, 0, 0)), pl.BlockSpec((1, k, n), lambda i: (i, 0, 0))],
        out_specs=pl.BlockSpec((1, m, n), lambda i: (i, 0, 0)),
        out_shape=jax.ShapeDtypeStruct((h, m, n), F32),
        compiler_params=_params(("parallel",)),
        name="weight_fold",
    )(a, b)


def _ffn_body(x_ref, gpre_ref, gpost_ref, wg_ref, wu_ref, wo_ref, o_ref, xn_ref, acc_ref, *, fc):
    x = x_ref[...]
    xn_ref[...] = _rms(x, gpre_ref[...]).astype(BF16)
    d_ff = wg_ref.shape[1]
    for c in range(d_ff // fc):
        xn = xn_ref[...]
        sl = slice(c * fc, (c + 1) * fc)
        gate = _dot(xn, wg_ref[:, sl])
        up = _dot(xn, wu_ref[:, sl])
        act = (gate * jax.nn.sigmoid(gate) * up).astype(BF16)
        part = _dot(act, wo_ref[sl, :])
        if c == 0:
            acc_ref[...] = part
        else:
            acc_ref[...] += part
    o_ref[...] = x + FFN_RES * _rms(acc_ref[...], gpost_ref[...])


def _ffn(x, g_pre, g_post, w_gate, w_up, w_out):
    n, d = x.shape
    d_ff = w_gate.shape[1]
    tm = _token_tile(n)
    fc = 256 if d_ff % 256 == 0 else d_ff
    row = pl.BlockSpec((tm, d), lambda i: (i, 0))
    return pl.pallas_call(
        functools.partial(_ffn_body, fc=fc),
        grid=(n // tm,),
        in_specs=[row, _const_spec((1, d)), _const_spec((1, d)), _const_spec((d, d_ff)), _const_spec((d, d_ff)),
                  _const_spec((d_ff, d))],
        out_specs=row,
        out_shape=jax.ShapeDtypeStruct((n, d), F32),
        scratch_shapes=[pltpu.VMEM((tm, d), BF16), pltpu.VMEM((tm, d), F32)],
        compiler_params=_params(("parallel",)),
        name="ffn",
    )(x, g_pre, g_post, w_gate, w_up, w_out)


_C_CQ = (0, 256)
_C_CKV = (256, 384)
_C_ROPE_A = (384, 512)
_C_ROPE_B = (512, 640)
_C_MLQK = (640, 1152)
_C_MLV = (1152, 1408)
_C_MLO = (1408, 1664)
_C_GATES = (1664, 1792)
_C_HG = (1792, 2816)
_IN_COLS = 2816


def _inproj_body(x_ref, g_ref, w_ref, qn_ref, kvn_ref, wq_ref, tabq_ref, cos_ref, sin_ref,
                 q_ref, kmat_ref, ckv_ref, kpe_ref, mlqk_ref, mlv_ref, mlo_ref, gates_ref, hg_ref, hn_ref):
    hn_ref[...] = _rms(x_ref[...], g_ref[...]).astype(BF16)

    def proj(cols):
        return _dot(hn_ref[...], w_ref[:, cols[0]:cols[1]])

    cqn = _rms(proj(_C_CQ), qn_ref[...]).astype(BF16)
    tab = tabq_ref[...]
    for h in range(MLA_HEADS):
        q_ref[h] = (_dot(cqn, wq_ref[:, h * QW:(h + 1) * QW]) * tab).astype(BF16)
    ckv = _rms(proj(_C_CKV), kvn_ref[...])
    ckv_ref[...] = ckv
    rot = proj(_C_ROPE_A) * cos_ref[...] + proj(_C_ROPE_B) * sin_ref[...]
    kpe_ref[...] = rot[:, :MLA_ROPE]
    kmat_ref[:, :MLA_KV_LORA] = ckv.astype(BF16)
    kmat_ref[:, MLA_KV_LORA:] = rot.astype(BF16)
    mlqk_ref[...] = proj(_C_MLQK)
    mlv_ref[...] = proj(_C_MLV)
    mlo_ref[...] = proj(_C_MLO)
    gates_ref[...] = proj(_C_GATES)
    hg_ref[...] = proj(_C_HG)


def _inproj(x, g, w_perm, q_norm, kv_norm, wq, tabq, cos_t, sin_t, seq):
    n, d = x.shape
    tm = _token_tile(n)
    if tm <= seq:
        assert seq % tm == 0
        nt = seq // tm
        tab_map = lambda i: (i % nt, 0)
    else:
        assert tm % seq == 0 and tabq.shape[0] == tm
        tab_map = lambda i: (0, 0)
    row = lambda w: pl.BlockSpec((tm, w), lambda i: (i, 0))
    tab = lambda w: pl.BlockSpec((tm, w), tab_map)
    outs = [
        (jax.ShapeDtypeStruct((MLA_HEADS, n, QW), BF16), pl.BlockSpec((MLA_HEADS, tm, QW), lambda i: (0, i, 0))),
        (jax.ShapeDtypeStruct((n, QW), BF16), row(QW)),
        (jax.ShapeDtypeStruct((n, MLA_KV_LORA), F32), row(MLA_KV_LORA)),
        (jax.ShapeDtypeStruct((n, MLA_ROPE), F32), row(MLA_ROPE)),
        (jax.ShapeDtypeStruct((n, 512), F32), row(512)),
        (jax.ShapeDtypeStruct((n, 256), F32), row(256)),
        (jax.ShapeDtypeStruct((n, 256), F32), row(256)),
        (jax.ShapeDtypeStruct((n, LANES), F32), row(LANES)),
        (jax.ShapeDtypeStruct((n, 1024), F32), row(1024)),
    ]
    return pl.pallas_call(
        _inproj_body,
        grid=(n // tm,),
        in_specs=[row(d), _const_spec((1, d)), _const_spec(w_perm.shape), _const_spec((1, MLA_Q_LORA)),
                  _const_spec((1, MLA_KV_LORA)), _const_spec(wq.shape), tab(QW), tab(LANES), tab(LANES)],
        out_specs=[o[1] for o in outs],
        out_shape=[o[0] for o in outs],
        scratch_shapes=[pltpu.VMEM((tm, d), BF16)],
        compiler_params=_params(("parallel",)),
        name="in_proj",
    )(x, g, w_perm, q_norm, kv_norm, wq, tabq, cos_t, sin_t)


def _mla_prompt_body(q_ref, k_ref, o_ref, acc_ref, *, tq):
    i = pl.program_id(1)
    rows = MLA_HEADS * tq
    q = q_ref[...].reshape(rows, QW)

    def block(j, m_prev, l_prev, masked):
        k = k_ref[0, pl.ds(pl.multiple_of(j * tq, tq), tq), :]
        s = _dot_nt(q, k)
        if masked:
            r = lax.broadcasted_iota(jnp.int32, (rows, tq), 0)
            c = lax.broadcasted_iota(jnp.int32, (rows, tq), 1)
            s = jnp.where((c // CHUNK) <= ((r % tq) // CHUNK), s, NEG_BIG)
        m_new = jnp.maximum(m_prev, jnp.max(s, axis=-1, keepdims=True))
        alpha = jnp.exp(m_prev - m_new)
        p = jnp.exp(s - m_new)
        l_new = alpha * l_prev + jnp.sum(p, axis=-1, keepdims=True)
        acc_ref[...] = alpha * acc_ref[...] + _dot(p.astype(BF16), k[:, :MLA_KV_LORA])
        return m_new, l_new

    acc_ref[...] = jnp.zeros_like(acc_ref)
    m0 = jnp.full((rows, 1), NEG_BIG, F32)
    l0 = jnp.zeros((rows, 1), F32)
    m1, l1 = lax.fori_loop(0, i, lambda j, c: block(j, c[0], c[1], False), (m0, l0))
    _, l2 = block(i, m1, l1, True)
    o = acc_ref[...] / l2
    for h in range(MLA_HEADS):
        o_ref[:, h * MLA_KV_LORA:(h + 1) * MLA_KV_LORA] = o[h * tq:(h + 1) * tq].astype(BF16)


def _mla_prompt(q, kmat, batch, seq):
    n = batch * seq
    tq = 256 if seq % 256 == 0 else seq
    assert tq % CHUNK == 0 or tq == seq
    nq = seq // tq
    return pl.pallas_call(
        functools.partial(_mla_prompt_body, tq=tq),
        grid=(batch, nq),
        in_specs=[pl.BlockSpec((MLA_HEADS, tq, QW), lambda b, i: (0, b * nq + i, 0)),
                  pl.BlockSpec((1, seq, QW), lambda b, i: (b, 0, 0))],
        out_specs=pl.BlockSpec((tq, MLA_HEADS * MLA_KV_LORA), lambda b, i: (b * nq + i, 0)),
        out_shape=jax.ShapeDtypeStruct((n, MLA_HEADS * MLA_KV_LORA), BF16),
        scratch_shapes=[pltpu.VMEM((MLA_HEADS * tq, MLA_KV_LORA), F32)],
        compiler_params=_params(("parallel", "arbitrary")),
        name="mla_prompt",
    )(q, kmat.reshape(batch, seq, QW))


def _mla_sample_body(*refs, tq, has_bias):
    if has_bias:
        q_ref, kc_ref, kn_ref, bc_ref, bn_ref, o_ref = refs
    else:
        q_ref, kc_ref, kn_ref, o_ref = refs
    rows = MLA_HEADS * tq
    q = q_ref[...].reshape(rows, QW)
    kc = kc_ref[0]
    kn = kn_ref[...]
    sc = _dot_nt(q, kc)
    sn = _dot_nt(q, kn)
    if has_bias:
        sc = sc + jnp.concatenate([bc_ref[...]] * MLA_HEADS, axis=0)
        sn = sn + jnp.concatenate([bn_ref[...]] * MLA_HEADS, axis=0)
    m = jnp.maximum(jnp.max(sc, axis=-1, keepdims=True), jnp.max(sn, axis=-1, keepdims=True))
    pc = jnp.exp(sc - m)
    pn = jnp.exp(sn - m)
    l = jnp.sum(pc, axis=-1, keepdims=True) + jnp.sum(pn, axis=-1, keepdims=True)
    o = (_dot(pc.astype(BF16), kc[:, :MLA_KV_LORA]) + _dot(pn.astype(BF16), kn[:, :MLA_KV_LORA])) / l
    for h in range(MLA_HEADS):
        o_ref[:, h * MLA_KV_LORA:(h + 1) * MLA_KV_LORA] = o[h * tq:(h + 1) * tq].astype(BF16)


def _mla_sample(q, kmat_new, kmat_cache, batch, seq, past_len):
    n = batch * seq
    q_pos = past_len + np.arange(seq)
    k_pos = np.arange(past_len + seq)
    mask = (k_pos[None, :] // CHUNK) <= (q_pos[:, None] // CHUNK)
    has_bias = not bool(mask.all())
    in_specs = [pl.BlockSpec((MLA_HEADS, seq, QW), lambda b: (0, b, 0)),
                pl.BlockSpec((1, past_len, QW), lambda b: (b, 0, 0)),
                pl.BlockSpec((seq, QW), lambda b: (b, 0))]
    args = [q, kmat_cache, kmat_new]
    if has_bias:
        bias = np.where(mask, 0.0, NEG_BIG).astype(np.float32)
        in_specs += [_const_spec((seq, past_len)), _const_spec((seq, seq))]
        args += [jnp.asarray(bias[:, :past_len]), jnp.asarray(bias[:, past_len:])]
    return pl.pallas_call(
        functools.partial(_mla_sample_body, tq=seq, has_bias=has_bias),
        grid=(batch,),
        in_specs=in_specs,
        out_specs=pl.BlockSpec((seq, MLA_HEADS * MLA_KV_LORA), lambda b: (b, 0)),
        out_shape=jax.ShapeDtypeStruct((n, MLA_HEADS * MLA_KV_LORA), BF16),
        compiler_params=_params(("parallel",)),
        name="mla_sample",
    )(*args)


def _mlstm_body(qk_ref, v_ref, og_ref, gates_ref, cw_ref, cb_ref, gb_ref, c0_ref, n0_ref, m0_ref, conv0_ref,
                out_ref, c1_ref, n1_ref, m1_ref, conv1_ref,
                xbuf, qk_s, g_s, cbd, n_s, m_s, *, tb, chunk):
    j = pl.program_id(1)
    H, DK, DV = ML_HEADS, ML_DK, ML_DV
    HD = H * DK
    L = chunk
    pad = 8

    @pl.when(j == 0)
    def _init():
        xbuf[0:pad, :] = conv0_ref[0]
        cbd[...] = jnp.zeros_like(cbd)
        for h in range(H):
            cbd[h * DK:(h + 1) * DK, h * DV:(h + 1) * DV] = c0_ref[0, h]
        n_s[...] = n0_ref[0]
        m_s[...] = m0_ref[0]

    xbuf[pad:pad + tb, :] = qk_ref[...]
    conv = cb_ref[...] + cw_ref[ML_CONV - 1:ML_CONV, :] * xbuf[pad:pad + tb, :]
    for t in range(1, ML_CONV):
        conv = conv + cw_ref[ML_CONV - 1 - t:ML_CONV - t, :] * xbuf[pad - t:pad - t + tb, :]
    act = conv * jax.nn.sigmoid(conv)
    lane = lax.broadcasted_iota(jnp.int32, (1, 2 * HD), 1)
    qk_s[...] = jnp.where(lane < HD, act, act * (DK ** -0.5))
    xbuf[0:pad, :] = xbuf[tb:tb + pad, :]
    gl = gates_ref[...] + gb_ref[...]
    lane_g = lax.broadcasted_iota(jnp.int32, (1, LANES), 1)
    g_s[...] = jnp.where(lane_g < H, gl, jnp.minimum(gl, 0.0) - jnp.log1p(jnp.exp(-jnp.abs(gl))))

    ri = lax.broadcasted_iota(jnp.int32, (L, L), 0)
    ci = lax.broadcasted_iota(jnp.int32, (L, L), 1)
    tri = (ci <= ri).astype(BF16)
    sel = (lax.broadcasted_iota(jnp.int32, (8, LANES), 0) == lax.broadcasted_iota(jnp.int32, (8, LANES), 1)).astype(BF16)
    row_l = lax.broadcasted_iota(jnp.int32, (L, H * L), 0)
    col_l = lax.broadcasted_iota(jnp.int32, (L, H * L), 1)
    causal = (col_l % L) <= row_l
    grp_d = _lane_group(HD, DK)
    rgrp = lax.broadcasted_iota(jnp.int32, (HD, 1), 0) // DK
    bd_mask = rgrp == _lane_group(H * DV, DV)

    def chunk_step(c, carry):
        r0 = pl.multiple_of(c * L, L)
        q = qk_s[pl.ds(r0, L), 0:HD]
        k = qk_s[pl.ds(r0, L), HD:2 * HD]
        v = v_ref[pl.ds(r0, L), :]
        g = g_s[pl.ds(r0, L), :]
        cs = _sel_dot(tri, g)
        x = jnp.where(lane_g < H, g, cs)
        xt = _sel_dot_nt(sel, x)
        li_c = [x[:, h:h + 1] for h in range(H)]
        b_c = [x[:, H + h:H + h + 1] for h in range(H)]
        m_prev = [m_s[:, h:h + 1] for h in range(H)]
        row_e = jnp.concatenate([xt[h:h + 1, :] - xt[H + h:H + h + 1, :] for h in range(H)], axis=1)
        logw = jnp.where(causal, _expand_cols(b_c, L) + row_e, -jnp.inf)
        m_intra = _seg_max(logw, L, H)
        log_inter = [b_c[h] + m_prev[h] for h in range(H)]
        m_t = [jnp.maximum(log_inter[h], m_intra[h]) for h in range(H)]
        w = jnp.exp(logw - _expand_cols(m_t, L))
        qb = q.astype(BF16)
        kexp = jnp.concatenate([jnp.where(grp_d == h, k, 0.0) for h in range(H)], axis=0).astype(BF16)
        vexp = jnp.concatenate([jnp.where(grp_d == h, v, 0.0) for h in range(H)], axis=0).astype(BF16)
        p = _dot_nt(qb, kexp) * w
        den_intra = _seg_sum(p, L, H)
        num = _dot(p.astype(BF16), vexp)
        w_inter = [jnp.exp(log_inter[h] - m_t[h]) for h in range(H)]
        num = num + _expand_cols(w_inter, DV) * _dot(qb, cbd[...].astype(BF16))
        qn = _seg_sum(q * n_s[...], DK, H)
        inv = [1.0 / jnp.maximum(jnp.abs(den_intra[h] + w_inter[h] * qn[h]), jnp.exp(-m_t[h])) for h in range(H)]
        hout = num * _expand_cols(inv, DV)
        og = og_ref[pl.ds(r0, L), :]
        out_ref[pl.ds(r0, L), :] = (jax.nn.sigmoid(og) * hout).astype(BF16)
        m_new = [m_t[h][L - 1:L, :] for h in range(H)]
        b_last = [b_c[h][L - 1:L, :] for h in range(H)]
        w_s = [jnp.exp(b_last[h] - b_c[h] + li_c[h] - m_new[h]) for h in range(H)]
        decay = [jnp.exp(b_last[h] + m_prev[h] - m_new[h]) for h in range(H)]
        kw = k * _expand_cols(w_s, DK)
        upd = _dot_tn(kw.astype(BF16), v.astype(BF16))
        dcol = decay[H - 1]
        for h in range(H - 2, -1, -1):
            dcol = jnp.where(rgrp == h, decay[h], dcol)
        cbd[...] = dcol * cbd[...] + jnp.where(bd_mask, upd, 0.0)
        n_s[...] = _expand_cols(decay, DK) * n_s[...] + jnp.sum(kw, axis=0, keepdims=True)
        m_row = m_s[...]
        for h in range(H):
            m_row = jnp.where(lane_g == h, m_new[h], m_row)
        m_s[...] = m_row
        return carry

    lax.fori_loop(0, tb // L, chunk_step, 0)

    @pl.when(j == pl.num_programs(1) - 1)
    def _fin():
        for h in range(H):
            c1_ref[0, h] = cbd[h * DK:(h + 1) * DK, h * DV:(h + 1) * DV]
        n1_ref[0] = n_s[...]
        m1_ref[0] = m_s[...]
        conv1_ref[0] = xbuf[0:pad, :]


def _mlstm(qk, v, og, gates, conv_w, conv_b, gbias, c0, n0, m0, conv0, batch, seq):
    n = batch * seq
    H, DK, DV = ML_HEADS, ML_DK, ML_DV
    chunk = min(CHUNK, seq)
    assert seq % chunk == 0 and seq >= ML_CONV - 1 and chunk % 8 == 0
    tb = 512 if seq % 512 == 0 else chunk
    nb = seq // tb
    pad = 8
    n0p = n0.reshape(batch, 1, H * DK)
    m0p = jnp.pad(m0.reshape(batch, 1, H), ((0, 0), (0, 0), (0, LANES - H)))
    conv0p = jnp.pad(conv0, ((0, 0), (pad - (ML_CONV - 1), 0), (0, 0)))
    row = lambda w: pl.BlockSpec((tb, w), lambda b, j: (b * nb + j, 0))
    per_b = lambda shape: pl.BlockSpec((1,) + shape, lambda b, j: (b,) + (0,) * len(shape))
    out, c1, n1, m1, conv1 = pl.pallas_call(
        functools.partial(_mlstm_body, tb=tb, chunk=chunk),
        grid=(batch, nb),
        in_specs=[row(2 * H * DK), row(H * DV), row(H * DV), row(LANES),
                  _const_spec((ML_CONV, 2 * H * DK)), _const_spec((1, 2 * H * DK)), _const_spec((1, LANES)),
                  per_b((H, DK, DV)), per_b((1, H * DK)), per_b((1, LANES)), per_b((pad, 2 * H * DK))],
        out_specs=[row(H * DV), per_b((H, DK, DV)), per_b((1, H * DK)), per_b((1, LANES)), per_b((pad, 2 * H * DK))],
        out_shape=[jax.ShapeDtypeStruct((n, H * DV), BF16), jax.ShapeDtypeStruct((batch, H, DK, DV), F32),
                   jax.ShapeDtypeStruct((batch, 1, H * DK), F32), jax.ShapeDtypeStruct((batch, 1, LANES), F32),
                   jax.ShapeDtypeStruct((batch, pad, 2 * H * DK), F32)],
        scratch_shapes=[pltpu.VMEM((tb + pad, 2 * H * DK), F32), pltpu.VMEM((tb, 2 * H * DK), F32),
                        pltpu.VMEM((tb, LANES), F32), pltpu.VMEM((H * DK, H * DV), F32),
                        pltpu.VMEM((1, H * DK), F32), pltpu.VMEM((1, LANES), F32)],
        compiler_params=_params(("parallel", "arbitrary")),
        name="mlstm",
    )(qk, v, og, gates, conv_w, conv_b, gbias, c0, n0p, m0p, conv0p)
    return (out, c1, n1.reshape(batch, H, DK), m1[:, 0, :H], conv1[:, pad - (ML_CONV - 1):, :])


def _hgrn_body(hg_ref, lbraw_ref, hnorm_ref, s0_ref, out_ref, s1_ref, hq_s, kk_s, lf_s, st, *, tb, sub, layer):
    j = pl.program_id(1)
    H, DK, DV = HG_HEADS, HG_DK, HG_DV
    HD = H * DK
    S = sub

    @pl.when(j == 0)
    def _init():
        st[...] = jnp.zeros_like(st)
        for h in range(H):
            st[h * DV:(h + 1) * DV, h * DK:(h + 1) * DK] = s0_ref[0, h].T

    raw = lbraw_ref[...]
    e = jnp.exp(raw - jnp.max(raw, axis=0, keepdims=True))
    sm = e / jnp.sum(e, axis=0, keepdims=True)
    cum = sm[0:1, :]
    for l in range(1, layer + 1):
        cum = cum + sm[l:l + 1, :]
    lb = cum - sm[0:1, :]

    z = hg_ref[:, HD:2 * HD]
    lsig = jnp.minimum(z, 0.0) - jnp.log1p(jnp.exp(-jnp.abs(z)))
    a1 = jnp.log(lb)
    a2 = jnp.log1p(-lb) + lsig
    delta = a1 - a2
    lf_s[...] = jnp.where(jnp.isnan(delta), a1 + a2, jnp.maximum(a1, a2) + jnp.log1p(jnp.exp(-jnp.abs(delta))))
    kk_s[...] = (1.0 - lb) * jax.nn.sigmoid(-z)
    xq = hg_ref[:, 0:HD]
    hq_s[...] = xq * jax.nn.sigmoid(xq)

    ri = lax.broadcasted_iota(jnp.int32, (S, S), 0)
    ci = lax.broadcasted_iota(jnp.int32, (S, S), 1)
    tri = (ci <= ri).astype(BF16)
    rowi = lax.broadcasted_iota(jnp.int32, (S, 1), 0)
    seg = ((lax.broadcasted_iota(jnp.int32, (HD, 1), 0) // DK) == _lane_group(H * DV, DV))
    seg_b = seg.astype(BF16)
    gnorm = hnorm_ref[...]

    def sub_step(c, carry):
        r0 = pl.multiple_of(c * S, S)
        hq = hq_s[pl.ds(r0, S), :]
        kk = kk_s[pl.ds(r0, S), :]
        b = _sel_dot(tri, lf_s[pl.ds(r0, S), :])
        iv = hg_ref[pl.ds(r0, S), 2 * HD:2 * HD + H * DV]
        gt = hg_ref[pl.ds(r0, S), 2 * HD + H * DV:2 * HD + 2 * H * DV]
        ws = []
        for s in range(S):
            dec = jnp.exp(jnp.where(rowi >= s, b - b[s:s + 1, :], -jnp.inf))
            ws.append(hq * dec * kk[s:s + 1, :])
        r = _dot(jnp.concatenate(ws, axis=0).astype(BF16), seg_b)
        o = _dot_nt((hq * jnp.exp(b)).astype(BF16), st[...].astype(BF16))
        for s in range(S):
            o = o + r[s * S:(s + 1) * S, :] * iv[s:s + 1, :]
        ms = _dot((o * o).astype(BF16), seg_b) * (1.0 / DV)
        y = o * lax.rsqrt(ms + EPS) * gnorm * (gt * jax.nn.sigmoid(gt))
        out_ref[pl.ds(r0, S), :] = y.astype(BF16)
        bl = b[S - 1:S, :]
        kd = kk * jnp.exp(bl - b)
        st[...] = jnp.exp(bl) * st[...] + jnp.where(seg, _dot_tn(iv.astype(BF16), kd.astype(BF16)), 0.0)
        return carry

    lax.fori_loop(0, tb // S, sub_step, 0)

    @pl.when(j == pl.num_programs(1) - 1)
    def _fin():
        for h in range(H):
            s1_ref[0, h] = st[h * DV:(h + 1) * DV, h * DK:(h + 1) * DK].T


def _hgrn(hg, lb_raw, hnorm, s0, batch, seq, layer):
    n = batch * seq
    H, DK, DV = HG_HEADS, HG_DK, HG_DV
    assert DK == DV
    sub = min(HG_SUB, seq)
    assert seq % sub == 0 and sub % 8 == 0
    tb = 512 if seq % 512 == 0 else sub
    nb = seq // tb
    depth = lb_raw.shape[0]
    per_b = pl.BlockSpec((1, H, DK, DV), lambda b, j: (b, 0, 0, 0))
    out, s1 = pl.pallas_call(
        functools.partial(_hgrn_body, tb=tb, sub=sub, layer=layer),
        grid=(batch, nb),
        in_specs=[pl.BlockSpec((tb, 4 * H * DK), lambda b, j: (b * nb + j, 0)), _const_spec((depth, H * DK)),
                  _const_spec((1, H * DV)), per_b],
        out_specs=[pl.BlockSpec((tb, H * DV), lambda b, j: (b * nb + j, 0)), per_b],
        out_shape=[jax.ShapeDtypeStruct((n, H * DV), BF16), jax.ShapeDtypeStruct((batch, H, DK, DV), F32)],
        scratch_shapes=[pltpu.VMEM((tb, H * DK), F32), pltpu.VMEM((tb, H * DK), F32), pltpu.VMEM((tb, H * DK), F32),
                        pltpu.VMEM((H * DV, H * DK), F32)],
        compiler_params=_params(("parallel", "arbitrary")),
        name="hgrn2",
    )(hg, lb_raw, hnorm, s0)
    return out, s1


def _outproj_body(x_ref, o_ref, ml_ref, hgo_ref, g_ref, w1_ref, w2_ref, w3_ref, y_ref):
    mix = _dot(o_ref[...], w1_ref[...]) + _dot(ml_ref[...], w2_ref[...]) + _dot(hgo_ref[...], w3_ref[...])
    y_ref[...] = x_ref[...] + _rms(mix, g_ref[...])


def _outproj(x, o_lat, ml_out, hg_out, g, w1, w2, w3):
    n, d = x.shape
    tm = _token_tile(n)
    row = lambda w: pl.BlockSpec((tm, w), lambda i: (i, 0))
    return pl.pallas_call(
        _outproj_body,
        grid=(n // tm,),
        in_specs=[row(d), row(o_lat.shape[1]), row(ml_out.shape[1]), row(hg_out.shape[1]), _const_spec((1, d)),
                  _const_spec(w1.shape), _const_spec(w2.shape), _const_spec(w3.shape)],
        out_specs=row(d),
        out_shape=jax.ShapeDtypeStruct((n, d), F32),
        compiler_params=_params(("parallel",)),
        name="out_proj",
    )(x, o_lat, ml_out, hg_out, g, w1, w2, w3)


def _rope_tables(pos):
    inv = ROPE_THETA ** (-jnp.arange(0, MLA_ROPE, 2, dtype=F32) / MLA_ROPE)
    ang = pos.astype(F32)[:, None] * inv[None, :]
    cos, sin = jnp.cos(ang), jnp.sin(ang)
    t = pos.shape[0]
    cos2, sin2 = jnp.concatenate([cos, cos], -1), jnp.concatenate([sin, sin], -1)
    tabq = MLA_SCALE * jnp.concatenate([jnp.ones((t, MLA_KV_LORA), F32), cos2, sin2,
                                        jnp.zeros((t, QW - MLA_KV_LORA - 2 * MLA_ROPE), F32)], -1)
    zpad = jnp.zeros((t, LANES - 2 * MLA_ROPE), F32)
    cos_t = jnp.concatenate([cos2, cos2, zpad], -1)
    sin_t = jnp.concatenate([sin2, sin2, zpad], -1)
    return tabq, cos_t, sin_t


def _swap_halves(w):
    half = w.shape[-1] // 2
    return jnp.concatenate([-w[..., half:], w[..., :half]], axis=-1)


def _layer_weights(l, ln_gains, w_ffn_in, w_ffn_out, w_in, w_out, mla_q_norm, mla_kv_norm, mla_w_uq, mla_w_uk,
                   mla_w_uv, ml_conv_w, ml_conv_b, ml_gate_bias, hg_norm):
    d = w_in.shape[1]
    d_ff = w_ffn_out.shape[2]
    sizes = (MLA_Q_LORA, MLA_KV_LORA, MLA_ROPE, 2 * ML_HEADS * ML_DK, ML_HEADS * ML_DV, ML_HEADS * ML_DV, ML_HEADS,
             ML_HEADS, HG_HEADS * HG_DK, HG_HEADS * HG_DK, HG_HEADS * HG_DV, HG_HEADS * HG_DV)
    pts = np.cumsum((0,) + sizes)
    part = [w_in[l][:, pts[i]:pts[i + 1]] for i in range(len(sizes))]
    zeros = lambda n: jnp.zeros((d, n), F32)
    kpe, kpe_sw = part[2], _swap_halves(part[2])
    w_perm = jnp.concatenate(
        [part[0], part[1], kpe, kpe, zeros(LANES - 2 * MLA_ROPE), kpe_sw, kpe_sw, zeros(LANES - 2 * MLA_ROPE),
         part[3], part[4], part[5], part[6], part[7], zeros(LANES - 2 * ML_HEADS), part[8], part[9], part[10],
         part[11]], axis=1).astype(BF16)
    assert w_perm.shape[1] == _IN_COLS
    uq = mla_w_uq[l].reshape(MLA_Q_LORA, MLA_HEADS, MLA_NOPE + MLA_ROPE)
    uq_nope = jnp.transpose(uq[:, :, :MLA_NOPE], (1, 0, 2))
    uk_t = jnp.transpose(mla_w_uk[l], (1, 2, 0))
    w_lat = _bmm(uq_nope, uk_t)
    uq_pe = jnp.transpose(uq[:, :, MLA_NOPE:], (1, 0, 2))
    wq = jnp.concatenate([w_lat, uq_pe, _swap_halves(uq_pe),
                          jnp.zeros((MLA_HEADS, MLA_Q_LORA, QW - MLA_KV_LORA - 2 * MLA_ROPE), F32)], axis=-1)
    wq = jnp.transpose(wq, (1, 0, 2)).reshape(MLA_Q_LORA, MLA_HEADS * QW).astype(BF16)
    n_mla = MLA_HEADS * MLA_V
    uv = jnp.transpose(mla_w_uv[l], (1, 0, 2))
    wo_mla = w_out[l][:n_mla].reshape(MLA_HEADS, MLA_V, d)
    w1 = _bmm(uv, wo_mla).reshape(MLA_HEADS * MLA_KV_LORA, d).astype(BF16)
    n_ml = ML_HEADS * ML_DV
    w2 = w_out[l][n_mla:n_mla + n_ml].astype(BF16)
    w3 = w_out[l][n_mla + n_ml:].astype(BF16)
    gb = ml_gate_bias[l]
    gbias = jnp.concatenate([gb[0], gb[1], jnp.zeros((LANES - 2 * ML_HEADS,), F32)])[None, :]
    ffn = []
    for j in range(2):
        wi = w_ffn_in[l, j]
        ffn.append((wi[:, :d_ff].astype(BF16), wi[:, d_ff:].astype(BF16), w_ffn_out[l, j].astype(BF16)))
    return dict(ln=ln_gains[l], ffn=ffn, w_perm=w_perm, wq=wq, w1=w1, w2=w2, w3=w3,
                q_norm=mla_q_norm[l][None, :], kv_norm=mla_kv_norm[l][None, :], conv_w=ml_conv_w[l],
                conv_b=ml_conv_b[l][None, :], gbias=gbias, hnorm=jnp.tile(hg_norm[l], HG_HEADS)[None, :])


def _tile_rows(t, reps):
    return jnp.tile(t, (reps, 1)) if reps > 1 else t


def _layer(x, lw, layer, lb_raw, batch, seq, pos, past):
    n, d = x.shape
    g = lw['ln']
    gain = lambda i: g[i][None, :]
    x = _ffn(x, gain(0), gain(1), *lw['ffn'][0])
    tabq, cos_t, sin_t = _rope_tables(pos)
    tm = _token_tile(n)
    reps = tm // seq if seq < tm else 1
    tabs = [_tile_rows(t, reps) for t in (tabq, cos_t, sin_t)]
    q, kmat, ckv, kpe, mlqk, mlv, mlo, gates, hg = _inproj(x, gain(2), lw['w_perm'], lw['q_norm'], lw['kv_norm'],
                                                          lw['wq'], *tabs, seq)
    if past is None:
        o_lat = _mla_prompt(q, kmat, batch, seq)
        c0 = jnp.zeros((batch, ML_HEADS, ML_DK, ML_DV), F32)
        n0 = jnp.zeros((batch, ML_HEADS, ML_DK), F32)
        m0 = jnp.zeros((batch, ML_HEADS), F32)
        conv0 = jnp.zeros((batch, ML_CONV - 1, 2 * ML_HEADS * ML_DK), F32)
        s0 = jnp.zeros((batch, HG_HEADS, HG_DK, HG_DV), F32)
    else:
        cache_ckv, cache_kpe, c0, n0, m0, conv0, s0 = past
        past_len = cache_ckv.shape[1]
        kmat_cache = jnp.concatenate(
            [cache_ckv, cache_kpe, cache_kpe,
             jnp.zeros((batch, past_len, QW - MLA_KV_LORA - 2 * MLA_ROPE), F32)], axis=-1).astype(BF16)
        o_lat = _mla_sample(q, kmat, kmat_cache, batch, seq, past_len)
    ml_out, c1, n1, m1, conv1 = _mlstm(mlqk, mlv, mlo, gates, lw['conv_w'], lw['conv_b'], lw['gbias'],
                                       c0, n0, m0, conv0, batch, seq)
    hg_out, s1 = _hgrn(hg, lb_raw, lw['hnorm'], s0, batch, seq, layer)
    x = _outproj(x, o_lat, ml_out, hg_out, gain(3), lw['w1'], lw['w2'], lw['w3'])
    x = _ffn(x, gain(4), gain(5), *lw['ffn'][1])
    state = (ckv.reshape(batch, seq, MLA_KV_LORA), kpe.reshape(batch, seq, MLA_ROPE), c1, n1, m1, conv1, s1)
    return x, state


def kernel(x_prompt, x_sample, cache_ckv, cache_kpe, state_mlstm_c, state_mlstm_n, state_mlstm_m, state_mlstm_conv, state_hgrn, ln_gains, w_ffn_in, w_ffn_out, w_in, w_out, mla_q_norm, mla_kv_norm, mla_w_uq, mla_w_uk, mla_w_uv, ml_conv_w, ml_conv_b, ml_gate_bias, hg_lb_raw, hg_norm):
    depth = w_in.shape[0]
    bp, tp, d = x_prompt.shape
    bs, ts, _ = x_sample.shape
    past_len = cache_ckv.shape[2]
    pos_p = jnp.arange(tp)
    pos_s = past_len + jnp.arange(ts)
    yp = x_prompt.reshape(bp * tp, d)
    ys = x_sample.reshape(bs * ts, d)
    lb_raw = hg_lb_raw.astype(F32)
    p_states, s_states = [], []
    for l in range(depth):
        lw = _layer_weights(l, ln_gains, w_ffn_in, w_ffn_out, w_in, w_out, mla_q_norm, mla_kv_norm, mla_w_uq,
                            mla_w_uk, mla_w_uv, ml_conv_w, ml_conv_b, ml_gate_bias, hg_norm)
        yp, st_p = _layer(yp, lw, l, lb_raw, bp, tp, pos_p, None)
        past = (cache_ckv[l], cache_kpe[l], state_mlstm_c[l], state_mlstm_n[l], state_mlstm_m[l],
                state_mlstm_conv[l], state_hgrn[l])
        ys, st_s = _layer(ys, lw, l, lb_raw, bs, ts, pos_s, past)
        p_states.append(st_p)
        s_states.append(st_s)
    p_out = [jnp.stack([st[i] for st in p_states]) for i in range(7)]
    s_out = [jnp.stack([st[i] for st in s_states]) for i in range(7)]
    return (yp.reshape(bp, tp, d), ys.reshape(bs, ts, d), *p_out, *s_out)
```

```python
import functools

import numpy as np
import jax
import jax.numpy as jnp
from jax import lax
from jax.experimental import pallas as pl
from jax.experimental.pallas import tpu as pltpu

F32 = jnp.float32
BF16 = jnp.bfloat16

CHUNK = 64
FFN_RES = 0.5
EPS = 1e-6
MLA_HEADS = 8
MLA_NOPE = 64
MLA_ROPE = 32
MLA_V = 64
MLA_Q_LORA = 256
MLA_KV_LORA = 128
ROPE_THETA = 10000.0
MLA_SCALE = (MLA_NOPE + MLA_ROPE) ** -0.5
ML_HEADS = 4
ML_DK = 64
ML_DV = 64
ML_CONV = 4
HG_HEADS = 4
HG_DK = 64
HG_DV = 64

LANES = 128
VMEM_LIMIT = 56 * 1024 * 1024
QW = 2 * LANES
_K_ONE = MLA_KV_LORA + 2 * MLA_ROPE
LOG2E = 1.4426950408889634
HG_SUB = 16
NEG_BIG = -1e30


def _rms(x, g):
    return x * lax.rsqrt(jnp.mean(x * x, axis=-1, keepdims=True) + EPS) * g


def _dot(a, b):
    return jnp.dot(a, b, preferred_element_type=F32)


def _dot_nt(a, b):
    return lax.dot_general(a, b, (((1,), (1,)), ((), ())), preferred_element_type=F32)


def _dot_tn(a, b):
    return lax.dot_general(a, b, (((0,), (0,)), ((), ())), preferred_element_type=F32)


def _split3(x):
    hi = x.astype(BF16)
    r = x - hi.astype(F32)
    mid = r.astype(BF16)
    lo = (r - mid.astype(F32)).astype(BF16)
    return hi, mid, lo


def _sel_dot(sel, x):
    hi, mid, lo = _split3(x)
    return _dot(sel, hi) + _dot(sel, mid) + _dot(sel, lo)


def _sel_dot_nt(sel, x):
    hi, mid, lo = _split3(x)
    return _dot_nt(sel, hi) + _dot_nt(sel, mid) + _dot_nt(sel, lo)


def _lane_group(n, g):
    return lax.broadcasted_iota(jnp.int32, (1, n), 1) // g


def _expand_cols(cols, g):
    grp = _lane_group(len(cols) * g, g)
    out = cols[-1]
    for h in range(len(cols) - 2, -1, -1):
        out = jnp.where(grp == h, cols[h], out)
    return out


def _seg_sum(x, g, heads):
    grp = _lane_group(heads * g, g)
    return [jnp.sum(jnp.where(grp == h, x, 0.0), axis=-1, keepdims=True) for h in range(heads)]


def _seg_max(x, g, heads):
    grp = _lane_group(heads * g, g)
    return [jnp.max(jnp.where(grp == h, x, -jnp.inf), axis=-1, keepdims=True) for h in range(heads)]


def _const_spec(shape):
    nd = len(shape)
    return pl.BlockSpec(shape, lambda *_: (0,) * nd, pipeline_mode=pl.Buffered(1))


def _params(sem):
    return pltpu.CompilerParams(dimension_semantics=sem, vmem_limit_bytes=VMEM_LIMIT)


def _token_tile(n):
    for t in (512, 256, 128, 64, 32, 16, 8):
        if n % t == 0:
            return t
    raise ValueError(f"token count {n} must be a multiple of 8")


def _bmm_body(a_ref, b_ref, o_ref):
    o_ref[0] = jnp.dot(a_ref[0], b_ref[0], preferred_element_type=F32, precision=lax.Precision.HIGHEST)


def _bmm(a, b):
    h, m, k = a.shape
    n = b.shape[2]
    return pl.pallas_call(
        _bmm_body,
        grid=(h,),
        in_specs=[pl.BlockSpec((1, m, k), lambda i: (i, 0, 0)), pl.BlockSpec((1, k, n), lambda i: (i, 0, 0))],
        out_specs=pl.BlockSpec((1, m, n), lambda i: (i, 0, 0)),
        out_shape=jax.ShapeDtypeStruct((h, m, n), F32),
        compiler_params=_params(("parallel",)),
        name="weight_fold",
    )(a, b)


def _ffn_body(x_ref, gpre_ref, gpost_ref, wg_ref, wu_ref, wo_ref, o_ref, xn_ref, acc_ref, *, fc):
    x = x_ref[...]
    xn_ref[...] = _rms(x, gpre_ref[...]).astype(BF16)
    d_ff = wg_ref.shape[1]
    for c in range(d_ff // fc):
        xn = xn_ref[...]
        sl = slice(c * fc, (c + 1) * fc)
        gate = _dot(xn, wg_ref[:, sl])
        up = _dot(xn, wu_ref[:, sl])
        act = (gate * jax.nn.sigmoid(gate) * up).astype(BF16)
        part = _dot(act, wo_ref[sl, :])
        if c == 0:
            acc_ref[...] = part
        else:
            acc_ref[...] += part
    o_ref[...] = x + FFN_RES * _rms(acc_ref[...], gpost_ref[...])


def _ffn(x, g_pre, g_post, w_gate, w_up, w_out):
    n, d = x.shape
    d_ff = w_gate.shape[1]
    tm = _token_tile(n)
    fc = 256 if d_ff % 256 == 0 else d_ff
    row = pl.BlockSpec((tm, d), lambda i: (i, 0))
    return pl.pallas_call(
        functools.partial(_ffn_body, fc=fc),
        grid=(n // tm,),
        in_specs=[row, _const_spec((1, d)), _const_spec((1, d)), _const_spec((d, d_ff)), _const_spec((d, d_ff)),
                  _const_spec((d_ff, d))],
        out_specs=row,
        out_shape=jax.ShapeDtypeStruct((n, d), F32),
        scratch_shapes=[pltpu.VMEM((tm, d), BF16), pltpu.VMEM((tm, d), F32)],
        compiler_params=_params(("parallel",)),
        name="ffn",
    )(x, g_pre, g_post, w_gate, w_up, w_out)


_C_CQ = (0, 256)
_C_CKV = (256, 384)
_C_ROPE_A = (384, 512)
_C_ROPE_B = (512, 640)
_C_MLQK = (640, 1152)
_C_MLV = (1152, 1408)
_C_MLO = (1408, 1664)
_C_GATES = (1664, 1792)
_C_HG = (1792, 2816)
_IN_COLS = 2816


def _inproj_body(x_ref, g_ref, w_ref, qn_ref, kvn_ref, wq_ref, tabq_ref, cos_ref, sin_ref,
                 q_ref, kmat_ref, ckv_ref, kpe_ref, mlqk_ref, mlv_ref, mlo_ref, gates_ref, hg_ref, hn_ref):
    hn_ref[...] = _rms(x_ref[...], g_ref[...]).astype(BF16)

    def proj(cols):
        return _dot(hn_ref[...], w_ref[:, cols[0]:cols[1]])

    cqn = _rms(proj(_C_CQ), qn_ref[...]).astype(BF16)
    tab = tabq_ref[...]
    for h in range(MLA_HEADS):
        q_ref[h] = (_dot(cqn, wq_ref[:, h * QW:(h + 1) * QW]) * tab).astype(BF16)
    ckv = _rms(proj(_C_CKV), kvn_ref[...])
    ckv_ref[...] = ckv
    rot = proj(_C_ROPE_A) * cos_ref[...] + proj(_C_ROPE_B) * sin_ref[...]
    kpe_ref[...] = rot[:, :MLA_ROPE]
    kmat_ref[:, :MLA_KV_LORA] = ckv.astype(BF16)
    lane = lax.broadcasted_iota(jnp.int32, (1, LANES), 1)
    kmat_ref[:, MLA_KV_LORA:] = jnp.where(lane == _K_ONE - MLA_KV_LORA, 1.0, rot).astype(BF16)
    mlqk_ref[...] = proj(_C_MLQK)
    mlv_ref[...] = proj(_C_MLV)
    mlo_ref[...] = proj(_C_MLO)
    gates_ref[...] = proj(_C_GATES)
    hg_ref[...] = proj(_C_HG)


def _inproj(x, g, w_perm, q_norm, kv_norm, wq, tabq, cos_t, sin_t, seq):
    n, d = x.shape
    tm = _token_tile(n)
    if tm <= seq:
        assert seq % tm == 0
        nt = seq // tm
        tab_map = lambda i: (i % nt, 0)
    else:
        assert tm % seq == 0 and tabq.shape[0] == tm
        tab_map = lambda i: (0, 0)
    row = lambda w: pl.BlockSpec((tm, w), lambda i: (i, 0))
    tab = lambda w: pl.BlockSpec((tm, w), tab_map)
    outs = [
        (jax.ShapeDtypeStruct((MLA_HEADS, n, QW), BF16), pl.BlockSpec((MLA_HEADS, tm, QW), lambda i: (0, i, 0))),
        (jax.ShapeDtypeStruct((n, QW), BF16), row(QW)),
        (jax.ShapeDtypeStruct((n, MLA_KV_LORA), F32), row(MLA_KV_LORA)),
        (jax.ShapeDtypeStruct((n, MLA_ROPE), F32), row(MLA_ROPE)),
        (jax.ShapeDtypeStruct((n, 512), F32), row(512)),
        (jax.ShapeDtypeStruct((n, 256), F32), row(256)),
        (jax.ShapeDtypeStruct((n, 256), F32), row(256)),
        (jax.ShapeDtypeStruct((n, LANES), F32), row(LANES)),
        (jax.ShapeDtypeStruct((n, 1024), F32), row(1024)),
    ]
    return pl.pallas_call(
        _inproj_body,
        grid=(n // tm,),
        in_specs=[row(d), _const_spec((1, d)), _const_spec(w_perm.shape), _const_spec((1, MLA_Q_LORA)),
                  _const_spec((1, MLA_KV_LORA)), _const_spec(wq.shape), tab(QW), tab(LANES), tab(LANES)],
        out_specs=[o[1] for o in outs],
        out_shape=[o[0] for o in outs],
        scratch_shapes=[pltpu.VMEM((tm, d), BF16)],
        compiler_params=_params(("parallel",)),
        name="in_proj",
    )(x, g, w_perm, q_norm, kv_norm, wq, tabq, cos_t, sin_t)


def _mla_prompt_body(q_ref, k_ref, o_ref, acc_ref, s_ref, *, tq, tk):
    i = pl.program_id(1)
    j_last = (i * tq) // tk

    def keys(j):
        return k_ref[0, pl.ds(pl.multiple_of(j * tk, tk), tk), :]

    def consume(k, m_prev, k_next, shift):
        m_out = []
        for h in range(MLA_HEADS):
            s = s_ref[h]
            if shift is not None:
                r = lax.broadcasted_iota(jnp.int32, (tq, tk), 0)
                c = lax.broadcasted_iota(jnp.int32, (tq, tk), 1)
                s = jnp.where((c // CHUNK) - (r // CHUNK) <= shift, s, NEG_BIG)
            m_new = jnp.maximum(m_prev[h], jnp.max(s, axis=-1, keepdims=True))
            alpha = jnp.exp2(m_prev[h] - m_new)
            p = jnp.exp2(s - m_new).astype(BF16)
            acc_ref[h] = alpha * acc_ref[h] + _dot(p, k)
            if k_next is not None:
                s_ref[h] = _dot_nt(q_ref[h], k_next)
            m_out.append(m_new)
        return tuple(m_out)

    acc_ref[...] = jnp.zeros_like(acc_ref)
    k0 = keys(0)
    for h in range(MLA_HEADS):
        s_ref[h] = _dot_nt(q_ref[h], k0)
    m0 = tuple(jnp.full((tq, 1), NEG_BIG, F32) for _ in range(MLA_HEADS))
    m1 = lax.fori_loop(0, j_last, lambda j, m: consume(keys(j), m, keys(j + 1), None), m0)
    consume(keys(j_last), m1, None, (i * tq - j_last * tk) // CHUNK)
    for h in range(MLA_HEADS):
        acc = acc_ref[h]
        o = acc[:, :MLA_KV_LORA] * (1.0 / acc[:, _K_ONE:_K_ONE + 1])
        o_ref[:, h * MLA_KV_LORA:(h + 1) * MLA_KV_LORA] = o.astype(BF16)


def _mla_prompt(q, kmat, batch, seq):
    n = batch * seq
    tq = 256 if seq % 256 == 0 else seq
    tk = 2 * tq if seq % (2 * tq) == 0 else tq
    assert tq % CHUNK == 0 or tq == seq
    nq = seq // tq
    return pl.pallas_call(
        functools.partial(_mla_prompt_body, tq=tq, tk=tk),
        grid=(batch, nq),
        in_specs=[pl.BlockSpec((MLA_HEADS, tq, QW), lambda b, i: (0, b * nq + i, 0)),
                  pl.BlockSpec((1, seq, QW), lambda b, i: (b, 0, 0))],
        out_specs=pl.BlockSpec((tq, MLA_HEADS * MLA_KV_LORA), lambda b, i: (b * nq + i, 0)),
        out_shape=jax.ShapeDtypeStruct((n, MLA_HEADS * MLA_KV_LORA), BF16),
        scratch_shapes=[pltpu.VMEM((MLA_HEADS, tq, QW), F32), pltpu.VMEM((MLA_HEADS, tq, tk), F32)],
        compiler_params=_params(("parallel", "arbitrary")),
        name="mla_prompt",
    )(q, kmat.reshape(batch, seq, QW))


def _mla_sample_body(*refs, tq, has_bias):
    if has_bias:
        q_ref, kc_ref, kn_ref, bc_ref, bn_ref, o_ref = refs
    else:
        q_ref, kc_ref, kn_ref, o_ref = refs
    rows = MLA_HEADS * tq
    q = q_ref[...].reshape(rows, QW)
    kc = kc_ref[0]
    kn = kn_ref[...]
    sc = _dot_nt(q, kc)
    sn = _dot_nt(q, kn)
    if has_bias:
        sc = sc + jnp.concatenate([bc_ref[...]] * MLA_HEADS, axis=0)
        sn = sn + jnp.concatenate([bn_ref[...]] * MLA_HEADS, axis=0)
    m = jnp.maximum(jnp.max(sc, axis=-1, keepdims=True), jnp.max(sn, axis=-1, keepdims=True))
    pc = jnp.exp2(sc - m)
    pn = jnp.exp2(sn - m)
    l = jnp.sum(pc, axis=-1, keepdims=True) + jnp.sum(pn, axis=-1, keepdims=True)
    o = (_dot(pc.astype(BF16), kc[:, :MLA_KV_LORA]) + _dot(pn.astype(BF16), kn[:, :MLA_KV_LORA])) / l
    for h in range(MLA_HEADS):
        o_ref[:, h * MLA_KV_LORA:(h + 1) * MLA_KV_LORA] = o[h * tq:(h + 1) * tq].astype(BF16)


def _mla_sample(q, kmat_new, kmat_cache, batch, seq, past_len):
    n = batch * seq
    q_pos = past_len + np.arange(seq)
    k_pos = np.arange(past_len + seq)
    mask = (k_pos[None, :] // CHUNK) <= (q_pos[:, None] // CHUNK)
    has_bias = not bool(mask.all())
    in_specs = [pl.BlockSpec((MLA_HEADS, seq, QW), lambda b: (0, b, 0)),
                pl.BlockSpec((1, past_len, QW), lambda b: (b, 0, 0)),
                pl.BlockSpec((seq, QW), lambda b: (b, 0))]
    args = [q, kmat_cache, kmat_new]
    if has_bias:
        bias = np.where(mask, 0.0, NEG_BIG).astype(np.float32)
        in_specs += [_const_spec((seq, past_len)), _const_spec((seq, seq))]
        args += [jnp.asarray(bias[:, :past_len]), jnp.asarray(bias[:, past_len:])]
    return pl.pallas_call(
        functools.partial(_mla_sample_body, tq=seq, has_bias=has_bias),
        grid=(batch,),
        in_specs=in_specs,
        out_specs=pl.BlockSpec((seq, MLA_HEADS * MLA_KV_LORA), lambda b: (b, 0)),
        out_shape=jax.ShapeDtypeStruct((n, MLA_HEADS * MLA_KV_LORA), BF16),
        compiler_params=_params(("parallel",)),
        name="mla_sample",
    )(*args)


def _mlstm_body(qk_ref, v_ref, og_ref, gates_ref, cw_ref, cb_ref, gb_ref, c0_ref, n0_ref, m0_ref, conv0_ref,
                out_ref, c1_ref, n1_ref, m1_ref, conv1_ref,
                xbuf, qk_s, g_s, cbd, n_s, m_s, *, tb, chunk):
    j = pl.program_id(1)
    H, DK, DV = ML_HEADS, ML_DK, ML_DV
    HD = H * DK
    L = chunk
    pad = 8

    @pl.when(j == 0)
    def _init():
        xbuf[0:pad, :] = conv0_ref[0]
        cbd[...] = jnp.zeros_like(cbd)
        for h in range(H):
            cbd[h * DK:(h + 1) * DK, h * DV:(h + 1) * DV] = c0_ref[0, h]
        n_s[...] = n0_ref[0]
        m_s[...] = m0_ref[0]

    _mlstm_conv(qk_ref, cw_ref, cb_ref, xbuf, qk_s, tb, pad)
    gl = gates_ref[...] + gb_ref[...]
    lane_g = lax.broadcasted_iota(jnp.int32, (1, LANES), 1)
    g_s[...] = jnp.where(lane_g < H, gl, jnp.minimum(gl, 0.0) - jnp.log1p(jnp.exp(-jnp.abs(gl))))

    ri = lax.broadcasted_iota(jnp.int32, (L, L), 0)
    ci = lax.broadcasted_iota(jnp.int32, (L, L), 1)
    tri = (ci <= ri).astype(BF16)
    sel = (lax.broadcasted_iota(jnp.int32, (8, LANES), 0) == lax.broadcasted_iota(jnp.int32, (8, LANES), 1)).astype(BF16)
    row_l = lax.broadcasted_iota(jnp.int32, (L, H * L), 0)
    col_l = lax.broadcasted_iota(jnp.int32, (L, H * L), 1)
    causal = (col_l % L) <= row_l
    grp_d = _lane_group(HD, DK)
    rgrp = lax.broadcasted_iota(jnp.int32, (HD, 1), 0) // DK
    bd_mask = rgrp == _lane_group(H * DV, DV)

    def chunk_step(c, carry):
        r0 = pl.multiple_of(c * L, L)
        q = qk_s[pl.ds(r0, L), 0:HD]
        k = qk_s[pl.ds(r0, L), HD:2 * HD]
        v = v_ref[pl.ds(r0, L), :]
        g = g_s[pl.ds(r0, L), :]
        cs = _sel_dot(tri, g)
        x = jnp.where(lane_g < H, g, cs)
        xt = _sel_dot_nt(sel, x)
        li_c = [x[:, h:h + 1] for h in range(H)]
        b_c = [x[:, H + h:H + h + 1] for h in range(H)]
        m_prev = [m_s[:, h:h + 1] for h in range(H)]
        row_e = jnp.concatenate([xt[h:h + 1, :] - xt[H + h:H + h + 1, :] for h in range(H)], axis=1)
        logw = jnp.where(causal, _expand_cols(b_c, L) + row_e, -jnp.inf)
        m_intra = _seg_max(logw, L, H)
        log_inter = [b_c[h] + m_prev[h] for h in range(H)]
        m_t = [jnp.maximum(log_inter[h], m_intra[h]) for h in range(H)]
        w = jnp.exp(logw - _expand_cols(m_t, L))
        qb = q.astype(BF16)
        kexp = jnp.concatenate([jnp.where(grp_d == h, k, 0.0) for h in range(H)], axis=0).astype(BF16)
        vexp = jnp.concatenate([jnp.where(grp_d == h, v, 0.0) for h in range(H)], axis=0).astype(BF16)
        p = _dot_nt(qb, kexp) * w
        den_intra = _seg_sum(p, L, H)
        num = _dot(p.astype(BF16), vexp)
        w_inter = [jnp.exp(log_inter[h] - m_t[h]) for h in range(H)]
        num = num + _expand_cols(w_inter, DV) * _dot(qb, cbd[...].astype(BF16))
        qn = _seg_sum(q * n_s[...], DK, H)
        inv = [1.0 / jnp.maximum(jnp.abs(den_intra[h] + w_inter[h] * qn[h]), jnp.exp(-m_t[h])) for h in range(H)]
        hout = num * _expand_cols(inv, DV)
        og = og_ref[pl.ds(r0, L), :]
        out_ref[pl.ds(r0, L), :] = (jax.nn.sigmoid(og) * hout).astype(BF16)
        m_new = [m_t[h][L - 1:L, :] for h in range(H)]
        b_last = [b_c[h][L - 1:L, :] for h in range(H)]
        w_s = [jnp.exp(b_last[h] - b_c[h] + li_c[h] - m_new[h]) for h in range(H)]
        decay = [jnp.exp(b_last[h] + m_prev[h] - m_new[h]) for h in range(H)]
        kw = k * _expand_cols(w_s, DK)
        upd = _dot_tn(kw.astype(BF16), v.astype(BF16))
        dcol = decay[H - 1]
        for h in range(H - 2, -1, -1):
            dcol = jnp.where(rgrp == h, decay[h], dcol)
        cbd[...] = dcol * cbd[...] + jnp.where(bd_mask, upd, 0.0)
        n_s[...] = _expand_cols(decay, DK) * n_s[...] + jnp.sum(kw, axis=0, keepdims=True)
        m_row = m_s[...]
        for h in range(H):
            m_row = jnp.where(lane_g == h, m_new[h], m_row)
        m_s[...] = m_row
        return carry

    lax.fori_loop(0, tb // L, chunk_step, 0)

    @pl.when(j == pl.num_programs(1) - 1)
    def _fin():
        for h in range(H):
            c1_ref[0, h] = cbd[h * DK:(h + 1) * DK, h * DV:(h + 1) * DV]
        n1_ref[0] = n_s[...]
        m1_ref[0] = m_s[...]
        conv1_ref[0] = xbuf[0:pad, :]


def _mlstm_conv(qk_ref, cw_ref, cb_ref, xbuf, qk_s, tb, pad):
    HD = ML_HEADS * ML_DK
    xbuf[pad:pad + tb, :] = qk_ref[...]
    conv = cb_ref[...] + cw_ref[ML_CONV - 1:ML_CONV, :] * xbuf[pad:pad + tb, :]
    for t in range(1, ML_CONV):
        conv = conv + cw_ref[ML_CONV - 1 - t:ML_CONV - t, :] * xbuf[pad - t:pad - t + tb, :]
    act = conv * jax.nn.sigmoid(conv)
    lane = lax.broadcasted_iota(jnp.int32, (1, 2 * HD), 1)
    qk_s[...] = jnp.where(lane < HD, act, act * (ML_DK ** -0.5))
    xbuf[0:pad, :] = xbuf[tb:tb + pad, :]


def _mlstm_fast_body(qk_ref, v_ref, og_ref, gates_ref, cw_ref, cb_ref, gbc_ref, utri_ref, ecols_ref,
                     c0_ref, n0_ref, mc0_ref, mr0_ref, conv0_ref,
                     out_ref, c1_ref, n1_ref, m1_ref, conv1_ref,
                     xbuf, qk_s, y_s, u_s, cb_s, ks_s, dec_s, nr_s, num_s, den_s, wi_s, cbd, n_s, m_c, mrow_s, *, tb):
    j = pl.program_id(1)
    H, DK, DV = ML_HEADS, ML_DK, ML_DV
    HD = H * DK
    L = DK
    nc = tb // L
    pad = 8

    @pl.when(j == 0)
    def _init():
        xbuf[0:pad, :] = conv0_ref[0]
        cbd[...] = jnp.zeros_like(cbd)
        for h in range(H):
            cbd[h * DK:(h + 1) * DK, h * DV:(h + 1) * DV] = c0_ref[0, h]
        n_s[...] = n0_ref[0]
        m_c[...] = mc0_ref[0]
        mrow_s[...] = mr0_ref[0]

    _mlstm_conv(qk_ref, cw_ref, cb_ref, xbuf, qk_s, tb, pad)

    g8 = gates_ref[...].T[0:8, :] + gbc_ref[...]
    lsig = jnp.minimum(g8, 0.0) - jnp.log1p(jnp.exp(-jnp.abs(g8)))
    lf8 = pltpu.roll(lsig, 8 - H, axis=0)
    fh, fm, fl = _split3(lf8)
    b8 = _dot(fh, utri_ref[...]) + _dot(fm, utri_ref[...]) + _dot(fl, utri_ref[...])
    a8 = g8 - b8
    lane_t = lax.broadcasted_iota(jnp.int32, (1, tb), 1)
    cmax = a8
    sh = 1
    while sh < tb:
        cmax = jnp.maximum(cmax, jnp.where(lane_t >= sh, pltpu.roll(cmax, sh, axis=1), -jnp.inf))
        sh *= 2
    mp8 = jnp.maximum(m_c[:, 0:1], cmax)
    mt8 = b8 + mp8
    valid = lax.broadcasted_iota(jnp.int32, (8, 1), 0) < H
    x32 = jnp.concatenate([jnp.where(valid, x, 0.0) for x in (mp8, a8, jnp.exp(-mt8), mt8)], axis=0)
    xh, xm, xl = _split3(x32)
    ec = ecols_ref[...]
    y_s[...] = _dot_tn(xh, ec) + _dot_tn(xm, ec) + _dot_tn(xl, ec)
    m_c[...] = jnp.broadcast_to(mt8[:, tb - 1:tb], m_c.shape)

    grp = _lane_group(HD, DK)
    rgrp = lax.broadcasted_iota(jnp.int32, (HD, 1), 0) // DK
    bd_mask = rgrp == grp
    seg_b = bd_mask.astype(BF16)
    row_l = lax.broadcasted_iota(jnp.int32, (L, HD), 0)
    col_l = lax.broadcasted_iota(jnp.int32, (L, HD), 1) % L
    causal = col_l <= row_l
    diag = col_l == row_l

    for c in range(nc):
        r0 = c * L
        q = qk_s[r0:r0 + L, 0:HD]
        k = qk_s[r0:r0 + L, HD:2 * HD]
        vb = v_ref[r0:r0 + L, :].astype(BF16)
        m_e = y_s[r0:r0 + L, 0:HD]
        a_e = y_s[r0:r0 + L, HD:2 * HD]
        m_prev = mrow_s[...] if c == 0 else y_s[r0 - 1:r0, 0:HD]
        m_last = m_e[L - 1:L, :]
        row_a = jnp.sum(jnp.where(diag, a_e, 0.0), axis=0, keepdims=True)
        d = jnp.exp(jnp.where(causal, row_a - m_e, -jnp.inf))
        qb = q.astype(BF16)
        kb = k.astype(BF16)
        kexp = jnp.concatenate([jnp.where(grp == h, kb, jnp.zeros_like(kb)) for h in range(H)], axis=0)
        vexp = jnp.concatenate([jnp.where(grp == h, vb, jnp.zeros_like(vb)) for h in range(H)], axis=0)
        p = _dot_nt(qb, kexp) * d
        p_hi = p.astype(BF16)
        p_lo = (p - p_hi.astype(F32)).astype(BF16)
        num_s[r0:r0 + L, :] = _dot(p_hi, vexp)
        den_s[r0:r0 + L, :] = _dot(p_hi, seg_b) + _dot(p_lo, seg_b)
        wi_s[r0:r0 + L, :] = jnp.exp(m_prev - m_e)
        kw = k * jnp.exp(a_e - m_last)
        u_s[c] = jnp.where(bd_mask, _dot_tn(kw.astype(BF16), vb), 0.0)
        ks_s[c:c + 1, :] = jnp.sum(kw, axis=0, keepdims=True)
        dec_s[c:c + 1, :] = jnp.exp(m_prev - m_last)

    for c in range(nc):
        cb_s[c] = cbd[...].astype(BF16)
        nr_s[c:c + 1, :] = n_s[...]
        dec = dec_s[c:c + 1, :]
        cbd[...] = dec * cbd[...] + u_s[c]
        n_s[...] = dec * n_s[...] + ks_s[c:c + 1, :]

    for c in range(nc):
        r0 = c * L
        q = qk_s[r0:r0 + L, 0:HD]
        qn = q * nr_s[c:c + 1, :]
        qn_hi = qn.astype(BF16)
        qn_lo = (qn - qn_hi.astype(F32)).astype(BF16)
        w_inter = wi_s[r0:r0 + L, :]
        num = num_s[r0:r0 + L, :] + w_inter * _dot(q.astype(BF16), cb_s[c])
        den = den_s[r0:r0 + L, :] + w_inter * (_dot(qn_hi, seg_b) + _dot(qn_lo, seg_b))
        hout = num / jnp.maximum(jnp.abs(den), y_s[r0:r0 + L, 2 * HD:3 * HD])
        og = og_ref[r0:r0 + L, :]
        out_ref[r0:r0 + L, :] = (jax.nn.sigmoid(og) * hout).astype(BF16)

    mrow_s[...] = y_s[tb - 1:tb, 3 * HD:4 * HD]

    @pl.when(j == pl.num_programs(1) - 1)
    def _fin():
        for h in range(H):
            c1_ref[0, h] = cbd[h * DK:(h + 1) * DK, h * DV:(h + 1) * DV]
        n1_ref[0] = n_s[...]
        m1_ref[0] = m_c[...]
        conv1_ref[0] = xbuf[0:pad, :]


def _mlstm_fast(qk, v, og, gates, conv_w, conv_b, gate_bias, c0, n0, m0, conv0, batch, seq, tb):
    n = batch * seq
    H, DK, DV = ML_HEADS, ML_DK, ML_DV
    HD = H * DK
    nb = seq // tb
    nc = tb // DK
    pad = 8
    n0p = n0.reshape(batch, 1, HD)
    mc0 = jnp.broadcast_to(jnp.pad(m0, ((0, 0), (0, 8 - H)))[:, :, None], (batch, 8, LANES))
    mr0 = jnp.repeat(m0, DV, axis=1).reshape(batch, 1, HD)
    conv0p = jnp.pad(conv0, ((0, 0), (pad - (ML_CONV - 1), 0), (0, 0)))
    gbc = jnp.concatenate([gate_bias[0], gate_bias[1]])[:, None]
    idx = np.arange(tb)
    utri = jnp.asarray(idx[:, None] <= idx[None, :], BF16)
    ecols_np = np.zeros((32, 4 * HD), np.float32)
    for kq in range(4):
        for h in range(H):
            ecols_np[8 * kq + h, kq * HD + h * DV:kq * HD + (h + 1) * DV] = 1.0
    ecols = jnp.asarray(ecols_np, BF16)
    row = lambda w: pl.BlockSpec((tb, w), lambda b, j: (b * nb + j, 0))
    per_b = lambda shape: pl.BlockSpec((1,) + shape, lambda b, j: (b,) + (0,) * len(shape))
    f32 = lambda *shape: pltpu.VMEM(shape, F32)
    out, c1, n1, m1, conv1 = pl.pallas_call(
        functools.partial(_mlstm_fast_body, tb=tb),
        grid=(batch, nb),
        in_specs=[row(2 * HD), row(H * DV), row(H * DV), row(LANES),
                  _const_spec((ML_CONV, 2 * HD)), _const_spec((1, 2 * HD)), _const_spec((2 * H, 1)),
                  _const_spec((tb, tb)), _const_spec((32, 4 * HD)),
                  per_b((H, DK, DV)), per_b((1, HD)), per_b((8, LANES)), per_b((1, HD)), per_b((pad, 2 * HD))],
        out_specs=[row(H * DV), per_b((H, DK, DV)), per_b((1, HD)), per_b((8, LANES)), per_b((pad, 2 * HD))],
        out_shape=[jax.ShapeDtypeStruct((n, H * DV), BF16), jax.ShapeDtypeStruct((batch, H, DK, DV), F32),
                   jax.ShapeDtypeStruct((batch, 1, HD), F32), jax.ShapeDtypeStruct((batch, 8, LANES), F32),
                   jax.ShapeDtypeStruct((batch, pad, 2 * HD), F32)],
        scratch_shapes=[f32(tb + pad, 2 * HD), f32(tb, 2 * HD), f32(tb, 4 * HD), f32(nc, HD, H * DV),
                        pltpu.VMEM((nc, HD, H * DV), BF16), f32(nc, HD), f32(nc, HD), f32(nc, HD),
                        f32(tb, HD), f32(tb, HD), f32(tb, HD), f32(HD, H * DV), f32(1, HD), f32(8, LANES),
                        f32(1, HD)],
        compiler_params=_params(("parallel", "arbitrary")),
        name="mlstm_blocked",
    )(qk, v, og, gates, conv_w, conv_b, gbc, utri, ecols, c0, n0p, mc0, mr0, conv0p)
    return (out, c1, n1.reshape(batch, H, DK), m1[:, :H, 0], conv1[:, pad - (ML_CONV - 1):, :])


def _mlstm(qk, v, og, gates, conv_w, conv_b, gate_bias, c0, n0, m0, conv0, batch, seq):
    if ML_DK == ML_DV == CHUNK and seq % 512 == 0:
        return _mlstm_fast(qk, v, og, gates, conv_w, conv_b, gate_bias, c0, n0, m0, conv0, batch, seq, 512)
    gb = gate_bias
    gbias = jnp.concatenate([gb[0], gb[1], jnp.zeros((LANES - 2 * ML_HEADS,), F32)])[None, :]
    n = batch * seq
    H, DK, DV = ML_HEADS, ML_DK, ML_DV
    chunk = min(CHUNK, seq)
    assert seq % chunk == 0 and seq >= ML_CONV - 1 and chunk % 8 == 0
    tb = 512 if seq % 512 == 0 else chunk
    nb = seq // tb
    pad = 8
    n0p = n0.reshape(batch, 1, H * DK)
    m0p = jnp.pad(m0.reshape(batch, 1, H), ((0, 0), (0, 0), (0, LANES - H)))
    conv0p = jnp.pad(conv0, ((0, 0), (pad - (ML_CONV - 1), 0), (0, 0)))
    row = lambda w: pl.BlockSpec((tb, w), lambda b, j: (b * nb + j, 0))
    per_b = lambda shape: pl.BlockSpec((1,) + shape, lambda b, j: (b,) + (0,) * len(shape))
    out, c1, n1, m1, conv1 = pl.pallas_call(
        functools.partial(_mlstm_body, tb=tb, chunk=chunk),
        grid=(batch, nb),
        in_specs=[row(2 * H * DK), row(H * DV), row(H * DV), row(LANES),
                  _const_spec((ML_CONV, 2 * H * DK)), _const_spec((1, 2 * H * DK)), _const_spec((1, LANES)),
                  per_b((H, DK, DV)), per_b((1, H * DK)), per_b((1, LANES)), per_b((pad, 2 * H * DK))],
        out_specs=[row(H * DV), per_b((H, DK, DV)), per_b((1, H * DK)), per_b((1, LANES)), per_b((pad, 2 * H * DK))],
        out_shape=[jax.ShapeDtypeStruct((n, H * DV), BF16), jax.ShapeDtypeStruct((batch, H, DK, DV), F32),
                   jax.ShapeDtypeStruct((batch, 1, H * DK), F32), jax.ShapeDtypeStruct((batch, 1, LANES), F32),
                   jax.ShapeDtypeStruct((batch, pad, 2 * H * DK), F32)],
        scratch_shapes=[pltpu.VMEM((tb + pad, 2 * H * DK), F32), pltpu.VMEM((tb, 2 * H * DK), F32),
                        pltpu.VMEM((tb, LANES), F32), pltpu.VMEM((H * DK, H * DV), F32),
                        pltpu.VMEM((1, H * DK), F32), pltpu.VMEM((1, LANES), F32)],
        compiler_params=_params(("parallel", "arbitrary")),
        name="mlstm",
    )(qk, v, og, gates, conv_w, conv_b, gbias, c0, n0p, m0p, conv0p)
    return (out, c1, n1.reshape(batch, H, DK), m1[:, 0, :H], conv1[:, pad - (ML_CONV - 1):, :])


def _hgrn_body(hg_ref, lbraw_ref, hnorm_ref, tri_ref, s0_ref, out_ref, s1_ref, hq_s, kk_s, f_s, b_s, qe_s, st,
               *, tb, sub, layer, unroll):
    j = pl.program_id(1)
    H, DK, DV = HG_HEADS, HG_DK, HG_DV
    HD = H * DK
    S = sub

    @pl.when(j == 0)
    def _init():
        st[...] = jnp.zeros_like(st)
        for h in range(H):
            st[h * DV:(h + 1) * DV, h * DK:(h + 1) * DK] = s0_ref[0, h].T

    raw = lbraw_ref[...]
    e = jnp.exp(raw - jnp.max(raw, axis=0, keepdims=True))
    sm = e / jnp.sum(e, axis=0, keepdims=True)
    cum = sm[0:1, :]
    for l in range(1, layer + 1):
        cum = cum + sm[l:l + 1, :]
    lb = cum - sm[0:1, :]

    z = hg_ref[:, HD:2 * HD]
    lsig = jnp.minimum(z, 0.0) - jnp.log1p(jnp.exp(-jnp.abs(z)))
    a1 = jnp.log(lb)
    a2 = jnp.log1p(-lb) + lsig
    delta = a1 - a2
    lf = jnp.where(jnp.isnan(delta), a1 + a2, jnp.maximum(a1, a2) + jnp.log1p(jnp.exp(-jnp.abs(delta))))
    f_s[...] = jnp.exp(lf)
    b_all = _sel_dot(tri_ref[...], lf)
    b_s[...] = b_all
    kk_s[...] = (1.0 - lb) * jax.nn.sigmoid(-z)
    xq = hg_ref[:, 0:HD]
    hq_all = xq * jax.nn.sigmoid(xq)
    hq_s[...] = hq_all
    qe_s[...] = (hq_all * jnp.exp(b_all)).astype(BF16)

    rowi = lax.broadcasted_iota(jnp.int32, (S, 1), 0)
    seg = ((lax.broadcasted_iota(jnp.int32, (HD, 1), 0) // DK) == _lane_group(H * DV, DV))
    seg_b = seg.astype(BF16)
    gnorm = hnorm_ref[...]
    dec_last = jnp.broadcast_to(jnp.where(rowi == S - 1, 1.0, 0.0), (S, HD))

    def sub_step(c, carry):
        r0 = pl.multiple_of(c * S, S)
        hq = hq_s[pl.ds(r0, S), :]
        kk = kk_s[pl.ds(r0, S), :]
        f = f_s[pl.ds(r0, S), :]
        b = b_s[pl.ds(r0, S), :]
        iv = hg_ref[pl.ds(r0, S), 2 * HD:2 * HD + H * DV]
        gt = hg_ref[pl.ds(r0, S), 2 * HD + H * DV:2 * HD + 2 * H * DV]
        dec = dec_last
        ws = [None] * S
        ws[S - 1] = (hq * dec * kk[S - 1:S, :]).astype(BF16)
        for s in range(S - 2, -1, -1):
            dec = jnp.where(rowi == s, 1.0, dec * f[s + 1:s + 2, :])
            ws[s] = (hq * dec * kk[s:s + 1, :]).astype(BF16)
        r = _dot(jnp.concatenate(ws, axis=0), seg_b)
        o = _dot_nt(qe_s[pl.ds(r0, S), :], st[...].astype(BF16))
        for s in range(S):
            o = o + r[s * S:(s + 1) * S, :] * iv[s:s + 1, :]
        ms = _dot((o * o).astype(BF16), seg_b) * (1.0 / DV)
        y = o * lax.rsqrt(ms + EPS) * gnorm * (gt * jax.nn.sigmoid(gt))
        out_ref[pl.ds(r0, S), :] = y.astype(BF16)
        bl = b[S - 1:S, :]
        kd = kk * jnp.exp(bl - b)
        st[...] = jnp.exp(bl) * st[...] + jnp.where(seg, _dot_tn(iv.astype(BF16), kd.astype(BF16)), 0.0)
        return carry

    lax.fori_loop(0, tb // S, sub_step, 0, unroll=unroll)

    @pl.when(j == pl.num_programs(1) - 1)
    def _fin():
        for h in range(H):
            s1_ref[0, h] = st[h * DV:(h + 1) * DV, h * DK:(h + 1) * DK].T


def _hgrn(hg, lb_raw, hnorm, s0, batch, seq, layer):
    n = batch * seq
    H, DK, DV = HG_HEADS, HG_DK, HG_DV
    assert DK == DV
    sub = min(HG_SUB, seq)
    assert seq % sub == 0 and sub % 8 == 0
    tb = 512 if seq % 512 == 0 else sub
    nb = seq // tb
    depth = lb_raw.shape[0]
    n_sub = tb // sub
    unroll = 4 if n_sub % 4 == 0 else 1
    idx = np.arange(tb)
    tri = jnp.asarray((idx[:, None] // sub == idx[None, :] // sub) & (idx[None, :] <= idx[:, None]), BF16)
    per_b = pl.BlockSpec((1, H, DK, DV), lambda b, j: (b, 0, 0, 0))
    out, s1 = pl.pallas_call(
        functools.partial(_hgrn_body, tb=tb, sub=sub, layer=layer, unroll=unroll),
        grid=(batch, nb),
        in_specs=[pl.BlockSpec((tb, 4 * H * DK), lambda b, j: (b * nb + j, 0)), _const_spec((depth, H * DK)),
                  _const_spec((1, H * DV)), _const_spec((tb, tb)), per_b],
        out_specs=[pl.BlockSpec((tb, H * DV), lambda b, j: (b * nb + j, 0)), per_b],
        out_shape=[jax.ShapeDtypeStruct((n, H * DV), BF16), jax.ShapeDtypeStruct((batch, H, DK, DV), F32)],
        scratch_shapes=[pltpu.VMEM((tb, H * DK), F32), pltpu.VMEM((tb, H * DK), F32), pltpu.VMEM((tb, H * DK), F32),
                        pltpu.VMEM((tb, H * DK), F32), pltpu.VMEM((tb, H * DK), BF16),
                        pltpu.VMEM((H * DV, H * DK), F32)],
        compiler_params=_params(("parallel", "arbitrary")),
        name="hgrn2",
    )(hg, lb_raw, hnorm, tri, s0)
    return out, s1


def _outproj_body(x_ref, o_ref, ml_ref, hgo_ref, g_ref, w1_ref, w2_ref, w3_ref, y_ref):
    mix = _dot(o_ref[...], w1_ref[...]) + _dot(ml_ref[...], w2_ref[...]) + _dot(hgo_ref[...], w3_ref[...])
    y_ref[...] = x_ref[...] + _rms(mix, g_ref[...])


def _outproj(x, o_lat, ml_out, hg_out, g, w1, w2, w3):
    n, d = x.shape
    tm = _token_tile(n)
    row = lambda w: pl.BlockSpec((tm, w), lambda i: (i, 0))
    return pl.pallas_call(
        _outproj_body,
        grid=(n // tm,),
        in_specs=[row(d), row(o_lat.shape[1]), row(ml_out.shape[1]), row(hg_out.shape[1]), _const_spec((1, d)),
                  _const_spec(w1.shape), _const_spec(w2.shape), _const_spec(w3.shape)],
        out_specs=row(d),
        out_shape=jax.ShapeDtypeStruct((n, d), F32),
        compiler_params=_params(("parallel",)),
        name="out_proj",
    )(x, o_lat, ml_out, hg_out, g, w1, w2, w3)


def _rope_tables(pos):
    inv = ROPE_THETA ** (-jnp.arange(0, MLA_ROPE, 2, dtype=F32) / MLA_ROPE)
    ang = pos.astype(F32)[:, None] * inv[None, :]
    cos, sin = jnp.cos(ang), jnp.sin(ang)
    t = pos.shape[0]
    cos2, sin2 = jnp.concatenate([cos, cos], -1), jnp.concatenate([sin, sin], -1)
    tabq = (MLA_SCALE * LOG2E) * jnp.concatenate([jnp.ones((t, MLA_KV_LORA), F32), cos2, sin2,
                                        jnp.zeros((t, QW - MLA_KV_LORA - 2 * MLA_ROPE), F32)], -1)
    zpad = jnp.zeros((t, LANES - 2 * MLA_ROPE), F32)
    cos_t = jnp.concatenate([cos2, cos2, zpad], -1)
    sin_t = jnp.concatenate([sin2, sin2, zpad], -1)
    return tabq, cos_t, sin_t


def _swap_halves(w):
    half = w.shape[-1] // 2
    return jnp.concatenate([-w[..., half:], w[..., :half]], axis=-1)


def _layer_weights(l, ln_gains, w_ffn_in, w_ffn_out, w_in, w_out, mla_q_norm, mla_kv_norm, mla_w_uq, mla_w_uk,
                   mla_w_uv, ml_conv_w, ml_conv_b, ml_gate_bias, hg_norm):
    d = w_in.shape[1]
    d_ff = w_ffn_out.shape[2]
    sizes = (MLA_Q_LORA, MLA_KV_LORA, MLA_ROPE, 2 * ML_HEADS * ML_DK, ML_HEADS * ML_DV, ML_HEADS * ML_DV, ML_HEADS,
             ML_HEADS, HG_HEADS * HG_DK, HG_HEADS * HG_DK, HG_HEADS * HG_DV, HG_HEADS * HG_DV)
    pts = np.cumsum((0,) + sizes)
    part = [w_in[l][:, pts[i]:pts[i + 1]] for i in range(len(sizes))]
    zeros = lambda n: jnp.zeros((d, n), F32)
    kpe, kpe_sw = part[2], _swap_halves(part[2])
    w_perm = jnp.concatenate(
        [part[0], part[1], kpe, kpe, zeros(LANES - 2 * MLA_ROPE), kpe_sw, kpe_sw, zeros(LANES - 2 * MLA_ROPE),
         part[3], part[4], part[5], part[6], part[7], zeros(LANES - 2 * ML_HEADS), part[8], part[9], part[10],
         part[11]], axis=1).astype(BF16)
    assert w_perm.shape[1] == _IN_COLS
    uq = mla_w_uq[l].reshape(MLA_Q_LORA, MLA_HEADS, MLA_NOPE + MLA_ROPE)
    uq_nope = jnp.transpose(uq[:, :, :MLA_NOPE], (1, 0, 2))
    uk_t = jnp.transpose(mla_w_uk[l], (1, 2, 0))
    w_lat = _bmm(uq_nope, uk_t)
    uq_pe = jnp.transpose(uq[:, :, MLA_NOPE:], (1, 0, 2))
    wq = jnp.concatenate([w_lat, uq_pe, _swap_halves(uq_pe),
                          jnp.zeros((MLA_HEADS, MLA_Q_LORA, QW - MLA_KV_LORA - 2 * MLA_ROPE), F32)], axis=-1)
    wq = jnp.transpose(wq, (1, 0, 2)).reshape(MLA_Q_LORA, MLA_HEADS * QW).astype(BF16)
    n_mla = MLA_HEADS * MLA_V
    uv = jnp.transpose(mla_w_uv[l], (1, 0, 2))
    wo_mla = w_out[l][:n_mla].reshape(MLA_HEADS, MLA_V, d)
    w1 = _bmm(uv, wo_mla).reshape(MLA_HEADS * MLA_KV_LORA, d).astype(BF16)
    n_ml = ML_HEADS * ML_DV
    w2 = w_out[l][n_mla:n_mla + n_ml].astype(BF16)
    w3 = w_out[l][n_mla + n_ml:].astype(BF16)
    ffn = []
    for j in range(2):
        wi = w_ffn_in[l, j]
        ffn.append((wi[:, :d_ff].astype(BF16), wi[:, d_ff:].astype(BF16), w_ffn_out[l, j].astype(BF16)))
    return dict(ln=ln_gains[l], ffn=ffn, w_perm=w_perm, wq=wq, w1=w1, w2=w2, w3=w3,
                q_norm=mla_q_norm[l][None, :], kv_norm=mla_kv_norm[l][None, :], conv_w=ml_conv_w[l],
                conv_b=ml_conv_b[l][None, :], gate_bias=ml_gate_bias[l].astype(F32), hnorm=jnp.tile(hg_norm[l], HG_HEADS)[None, :])


def _tile_rows(t, reps):
    return jnp.tile(t, (reps, 1)) if reps > 1 else t


def _layer(x, lw, layer, lb_raw, batch, seq, pos, past):
    n, d = x.shape
    g = lw['ln']
    gain = lambda i: g[i][None, :]
    x = _ffn(x, gain(0), gain(1), *lw['ffn'][0])
    tabq, cos_t, sin_t = _rope_tables(pos)
    tm = _token_tile(n)
    reps = tm // seq if seq < tm else 1
    tabs = [_tile_rows(t, reps) for t in (tabq, cos_t, sin_t)]
    q, kmat, ckv, kpe, mlqk, mlv, mlo, gates, hg = _inproj(x, gain(2), lw['w_perm'], lw['q_norm'], lw['kv_norm'],
                                                          lw['wq'], *tabs, seq)
    if past is None:
        o_lat = _mla_prompt(q, kmat, batch, seq)
        c0 = jnp.zeros((batch, ML_HEADS, ML_DK, ML_DV), F32)
        n0 = jnp.zeros((batch, ML_HEADS, ML_DK), F32)
        m0 = jnp.zeros((batch, ML_HEADS), F32)
        conv0 = jnp.zeros((batch, ML_CONV - 1, 2 * ML_HEADS * ML_DK), F32)
        s0 = jnp.zeros((batch, HG_HEADS, HG_DK, HG_DV), F32)
    else:
        cache_ckv, cache_kpe, c0, n0, m0, conv0, s0 = past
        past_len = cache_ckv.shape[1]
        kmat_cache = jnp.concatenate(
            [cache_ckv, cache_kpe, cache_kpe,
             jnp.ones((batch, past_len, 1), F32),
             jnp.zeros((batch, past_len, QW - _K_ONE - 1), F32)], axis=-1).astype(BF16)
        o_lat = _mla_sample(q, kmat, kmat_cache, batch, seq, past_len)
    ml_out, c1, n1, m1, conv1 = _mlstm(mlqk, mlv, mlo, gates, lw['conv_w'], lw['conv_b'], lw['gate_bias'],
                                       c0, n0, m0, conv0, batch, seq)
    hg_out, s1 = _hgrn(hg, lb_raw, lw['hnorm'], s0, batch, seq, layer)
    x = _outproj(x, o_lat, ml_out, hg_out, gain(3), lw['w1'], lw['w2'], lw['w3'])
    x = _ffn(x, gain(4), gain(5), *lw['ffn'][1])
    state = (ckv.reshape(batch, seq, MLA_KV_LORA), kpe.reshape(batch, seq, MLA_ROPE), c1, n1, m1, conv1, s1)
    return x, state


def kernel(x_prompt, x_sample, cache_ckv, cache_kpe, state_mlstm_c, state_mlstm_n, state_mlstm_m, state_mlstm_conv, state_hgrn, ln_gains, w_ffn_in, w_ffn_out, w_in, w_out, mla_q_norm, mla_kv_norm, mla_w_uq, mla_w_uk, mla_w_uv, ml_conv_w, ml_conv_b, ml_gate_bias, hg_lb_raw, hg_norm):
    depth = w_in.shape[0]
    bp, tp, d = x_prompt.shape
    bs, ts, _ = x_sample.shape
    past_len = cache_ckv.shape[2]
    pos_p = jnp.arange(tp)
    pos_s = past_len + jnp.arange(ts)
    yp = x_prompt.reshape(bp * tp, d)
    ys = x_sample.reshape(bs * ts, d)
    lb_raw = hg_lb_raw.astype(F32)
    p_states, s_states = [], []
    for l in range(depth):
        lw = _layer_weights(l, ln_gains, w_ffn_in, w_ffn_out, w_in, w_out, mla_q_norm, mla_kv_norm, mla_w_uq,
                            mla_w_uk, mla_w_uv, ml_conv_w, ml_conv_b, ml_gate_bias, hg_norm)
        yp, st_p = _layer(yp, lw, l, lb_raw, bp, tp, pos_p, None)
        past = (cache_ckv[l], cache_kpe[l], state_mlstm_c[l], state_mlstm_n[l], state_mlstm_m[l],
                state_mlstm_conv[l], state_hgrn[l])
        ys, st_s = _layer(ys, lw, l, lb_raw, bs, ts, pos_s, past)
        p_states.append(st_p)
        s_states.append(st_s)
    p_out = [jnp.stack([st[i] for st in p_states]) for i in range(7)]
    s_out = [jnp.stack([st[i] for st in s_states]) for i in range(7)]
    return (yp.reshape(bp, tp, d), ys.reshape(bs, ts, d), *p_out, *s_out)
```

```python
import functools

import numpy as np
import jax
import jax.numpy as jnp
from jax import lax
from jax.experimental import pallas as pl
from jax.experimental.pallas import tpu as pltpu

F32 = jnp.float32
BF16 = jnp.bfloat16

CHUNK = 64
FFN_RES = 0.5
EPS = 1e-6
MLA_HEADS = 8
MLA_NOPE = 64
MLA_ROPE = 32
MLA_V = 64
MLA_Q_LORA = 256
MLA_KV_LORA = 128
ROPE_THETA = 10000.0
MLA_SCALE = (MLA_NOPE + MLA_ROPE) ** -0.5
ML_HEADS = 4
ML_DK = 64
ML_DV = 64
ML_CONV = 4
HG_HEADS = 4
HG_DK = 64
HG_DV = 64

LANES = 128
VMEM_LIMIT = 56 * 1024 * 1024
QW = 2 * LANES
_K_ONE = MLA_KV_LORA + 2 * MLA_ROPE
LOG2E = 1.4426950408889634
HG_SUB = 16
NEG_BIG = -1e30


def _rms(x, g):
    return x * lax.rsqrt(jnp.mean(x * x, axis=-1, keepdims=True) + EPS) * g


def _dot(a, b):
    return jnp.dot(a, b, preferred_element_type=F32)


def _dot_nt(a, b):
    return lax.dot_general(a, b, (((1,), (1,)), ((), ())), preferred_element_type=F32)


def _dot_tn(a, b):
    return lax.dot_general(a, b, (((0,), (0,)), ((), ())), preferred_element_type=F32)


def _split3(x):
    hi = x.astype(BF16)
    r = x - hi.astype(F32)
    mid = r.astype(BF16)
    lo = (r - mid.astype(F32)).astype(BF16)
    return hi, mid, lo


def _sel_dot(sel, x):
    hi, mid, lo = _split3(x)
    return _dot(sel, hi) + _dot(sel, mid) + _dot(sel, lo)


def _sel_dot_nt(sel, x):
    hi, mid, lo = _split3(x)
    return _dot_nt(sel, hi) + _dot_nt(sel, mid) + _dot_nt(sel, lo)


def _lane_group(n, g):
    return lax.broadcasted_iota(jnp.int32, (1, n), 1) // g


def _expand_cols(cols, g):
    grp = _lane_group(len(cols) * g, g)
    out = cols[-1]
    for h in range(len(cols) - 2, -1, -1):
        out = jnp.where(grp == h, cols[h], out)
    return out


def _seg_sum(x, g, heads):
    grp = _lane_group(heads * g, g)
    return [jnp.sum(jnp.where(grp == h, x, 0.0), axis=-1, keepdims=True) for h in range(heads)]


def _seg_max(x, g, heads):
    grp = _lane_group(heads * g, g)
    return [jnp.max(jnp.where(grp == h, x, -jnp.inf), axis=-1, keepdims=True) for h in range(heads)]


def _const_spec(shape):
    nd = len(shape)
    return pl.BlockSpec(shape, lambda *_: (0,) * nd, pipeline_mode=pl.Buffered(1))


def _params(sem):
    return pltpu.CompilerParams(dimension_semantics=sem, vmem_limit_bytes=VMEM_LIMIT)


def _token_tile(n):
    for t in (512, 256, 128, 64, 32, 16, 8):
        if n % t == 0:
            return t
    raise ValueError(f"token count {n} must be a multiple of 8")


def _bmm_body(a_ref, b_ref, o_ref):
    o_ref[0] = jnp.dot(a_ref[0], b_ref[0], preferred_element_type=F32, precision=lax.Precision.HIGHEST)


def _bmm(a, b):
    h, m, k = a.shape
    n = b.shape[2]
    return pl.pallas_call(
        _bmm_body,
        grid=(h,),
        in_specs=[pl.BlockSpec((1, m, k), lambda i: (i, 0, 0)), pl.BlockSpec((1, k, n), lambda i: (i, 0, 0))],
        out_specs=pl.BlockSpec((1, m, n), lambda i: (i, 0, 0)),
        out_shape=jax.ShapeDtypeStruct((h, m, n), F32),
        compiler_params=_params(("parallel",)),
        name="weight_fold",
    )(a, b)


def _ffn_apply(x, gpre_ref, gpost_ref, wg_ref, wu_ref, wo_ref, xn_ref, acc_ref, fc):
    xn_ref[...] = _rms(x, gpre_ref[...]).astype(BF16)
    d_ff = wg_ref.shape[1]
    for c in range(d_ff // fc):
        xn = xn_ref[...]
        sl = slice(c * fc, (c + 1) * fc)
        gate = _dot(xn, wg_ref[:, sl])
        up = _dot(xn, wu_ref[:, sl])
        act = (gate * jax.nn.sigmoid(gate) * up).astype(BF16)
        part = _dot(act, wo_ref[sl, :])
        if c == 0:
            acc_ref[...] = part
        else:
            acc_ref[...] += part
    return x + FFN_RES * _rms(acc_ref[...], gpost_ref[...])


def _ffn_body(x_ref, gpre_ref, gpost_ref, wg_ref, wu_ref, wo_ref, o_ref, xn_ref, acc_ref, *, fc):
    o_ref[...] = _ffn_apply(x_ref[...], gpre_ref, gpost_ref, wg_ref, wu_ref, wo_ref, xn_ref, acc_ref, fc)


def _ffn(x, g_pre, g_post, w_gate, w_up, w_out):
    n, d = x.shape
    d_ff = w_gate.shape[1]
    tm = _token_tile(n)
    fc = 256 if d_ff % 256 == 0 else d_ff
    row = pl.BlockSpec((tm, d), lambda i: (i, 0))
    return pl.pallas_call(
        functools.partial(_ffn_body, fc=fc),
        grid=(n // tm,),
        in_specs=[row, _const_spec((1, d)), _const_spec((1, d)), _const_spec((d, d_ff)), _const_spec((d, d_ff)),
                  _const_spec((d_ff, d))],
        out_specs=row,
        out_shape=jax.ShapeDtypeStruct((n, d), F32),
        scratch_shapes=[pltpu.VMEM((tm, d), BF16), pltpu.VMEM((tm, d), F32)],
        compiler_params=_params(("parallel",)),
        name="ffn",
    )(x, g_pre, g_post, w_gate, w_up, w_out)


_C_CQ = (0, 256)
_C_CKV = (256, 384)
_C_ROPE_A = (384, 512)
_C_ROPE_B = (512, 640)
_C_MLQK = (640, 1152)
_C_MLV = (1152, 1408)
_C_MLO = (1408, 1664)
_C_GATES = (1664, 1792)
_C_HG = (1792, 2816)
_IN_COLS = 2816


def _inproj_body(x_ref, g_ref, w_ref, qn_ref, kvn_ref, wq_ref, tabq_ref, cos_ref, sin_ref,
                 q_ref, kmat_ref, ckv_ref, kpe_ref, mlqk_ref, mlv_ref, mlo_ref, gates_ref, hg_ref, hn_ref):
    hn_ref[...] = _rms(x_ref[...], g_ref[...]).astype(BF16)

    def proj(cols):
        return _dot(hn_ref[...], w_ref[:, cols[0]:cols[1]])

    cqn = _rms(proj(_C_CQ), qn_ref[...]).astype(BF16)
    tab = tabq_ref[...]
    for h in range(MLA_HEADS):
        q_ref[h] = (_dot(cqn, wq_ref[:, h * QW:(h + 1) * QW]) * tab).astype(BF16)
    ckv = _rms(proj(_C_CKV), kvn_ref[...])
    ckv_ref[...] = ckv
    rot = proj(_C_ROPE_A) * cos_ref[...] + proj(_C_ROPE_B) * sin_ref[...]
    kpe_ref[...] = rot[:, :MLA_ROPE]
    kmat_ref[:, :MLA_KV_LORA] = ckv.astype(BF16)
    lane = lax.broadcasted_iota(jnp.int32, (1, LANES), 1)
    kmat_ref[:, MLA_KV_LORA:] = jnp.where(lane == _K_ONE - MLA_KV_LORA, 1.0, rot).astype(BF16)
    mlqk_ref[...] = proj(_C_MLQK)
    mlv_ref[...] = proj(_C_MLV)
    mlo_ref[...] = proj(_C_MLO)
    gates_ref[...] = proj(_C_GATES)
    hg_ref[...] = proj(_C_HG)


def _inproj(x, g, w_perm, q_norm, kv_norm, wq, tabq, cos_t, sin_t, seq):
    n, d = x.shape
    tm = _token_tile(n)
    if tm <= seq:
        assert seq % tm == 0
        nt = seq // tm
        tab_map = lambda i: (i % nt, 0)
    else:
        assert tm % seq == 0 and tabq.shape[0] == tm
        tab_map = lambda i: (0, 0)
    row = lambda w: pl.BlockSpec((tm, w), lambda i: (i, 0))
    tab = lambda w: pl.BlockSpec((tm, w), tab_map)
    outs = [
        (jax.ShapeDtypeStruct((MLA_HEADS, n, QW), BF16), pl.BlockSpec((MLA_HEADS, tm, QW), lambda i: (0, i, 0))),
        (jax.ShapeDtypeStruct((n, QW), BF16), row(QW)),
        (jax.ShapeDtypeStruct((n, MLA_KV_LORA), F32), row(MLA_KV_LORA)),
        (jax.ShapeDtypeStruct((n, MLA_ROPE), F32), row(MLA_ROPE)),
        (jax.ShapeDtypeStruct((n, 512), F32), row(512)),
        (jax.ShapeDtypeStruct((n, 256), F32), row(256)),
        (jax.ShapeDtypeStruct((n, 256), F32), row(256)),
        (jax.ShapeDtypeStruct((n, LANES), F32), row(LANES)),
        (jax.ShapeDtypeStruct((n, 1024), F32), row(1024)),
    ]
    return pl.pallas_call(
        _inproj_body,
        grid=(n // tm,),
        in_specs=[row(d), _const_spec((1, d)), _const_spec(w_perm.shape), _const_spec((1, MLA_Q_LORA)),
                  _const_spec((1, MLA_KV_LORA)), _const_spec(wq.shape), tab(QW), tab(LANES), tab(LANES)],
        out_specs=[o[1] for o in outs],
        out_shape=[o[0] for o in outs],
        scratch_shapes=[pltpu.VMEM((tm, d), BF16)],
        compiler_params=_params(("parallel",)),
        name="in_proj",
    )(x, g, w_perm, q_norm, kv_norm, wq, tabq, cos_t, sin_t)


def _mla_prompt_body(q_ref, k_ref, o_ref, acc_ref, s_ref, *, tq, tk):
    i = pl.program_id(1)
    j_last = (i * tq) // tk

    def keys(j):
        return k_ref[0, pl.ds(pl.multiple_of(j * tk, tk), tk), :]

    def consume(k, m_prev, k_next, shift):
        m_out = []
        for h in range(MLA_HEADS):
            s = s_ref[h]
            if shift is not None:
                r = lax.broadcasted_iota(jnp.int32, (tq, tk), 0)
                c = lax.broadcasted_iota(jnp.int32, (tq, tk), 1)
                s = jnp.where((c // CHUNK) - (r // CHUNK) <= shift, s, NEG_BIG)
            m_new = jnp.maximum(m_prev[h], jnp.max(s, axis=-1, keepdims=True))
            alpha = jnp.exp2(m_prev[h] - m_new)
            p = jnp.exp2(s - m_new).astype(BF16)
            acc_ref[h] = alpha * acc_ref[h] + _dot(p, k)
            if k_next is not None:
                s_ref[h] = _dot_nt(q_ref[h], k_next)
            m_out.append(m_new)
        return tuple(m_out)

    acc_ref[...] = jnp.zeros_like(acc_ref)
    k0 = keys(0)
    for h in range(MLA_HEADS):
        s_ref[h] = _dot_nt(q_ref[h], k0)
    m0 = tuple(jnp.full((tq, 1), NEG_BIG, F32) for _ in range(MLA_HEADS))
    m1 = lax.fori_loop(0, j_last, lambda j, m: consume(keys(j), m, keys(j + 1), None), m0)
    consume(keys(j_last), m1, None, (i * tq - j_last * tk) // CHUNK)
    for h in range(MLA_HEADS):
        acc = acc_ref[h]
        o = acc[:, :MLA_KV_LORA] * (1.0 / acc[:, _K_ONE:_K_ONE + 1])
        o_ref[:, h * MLA_KV_LORA:(h + 1) * MLA_KV_LORA] = o.astype(BF16)


def _mla_prompt(q, kmat, batch, seq):
    n = batch * seq
    tq = 256 if seq % 256 == 0 else seq
    tk = 2 * tq if seq % (2 * tq) == 0 else tq
    assert tq % CHUNK == 0 or tq == seq
    nq = seq // tq
    return pl.pallas_call(
        functools.partial(_mla_prompt_body, tq=tq, tk=tk),
        grid=(batch, nq),
        in_specs=[pl.BlockSpec((MLA_HEADS, tq, QW), lambda b, i: (0, b * nq + i, 0)),
                  pl.BlockSpec((1, seq, QW), lambda b, i: (b, 0, 0))],
        out_specs=pl.BlockSpec((tq, MLA_HEADS * MLA_KV_LORA), lambda b, i: (b * nq + i, 0)),
        out_shape=jax.ShapeDtypeStruct((n, MLA_HEADS * MLA_KV_LORA), BF16),
        scratch_shapes=[pltpu.VMEM((MLA_HEADS, tq, QW), F32), pltpu.VMEM((MLA_HEADS, tq, tk), F32)],
        compiler_params=_params(("parallel", "arbitrary")),
        name="mla_prompt",
    )(q, kmat.reshape(batch, seq, QW))


def _mla_sample_body(*refs, tq, has_bias):
    if has_bias:
        q_ref, kc_ref, kn_ref, bc_ref, bn_ref, o_ref = refs
    else:
        q_ref, kc_ref, kn_ref, o_ref = refs
    rows = MLA_HEADS * tq
    q = q_ref[...].reshape(rows, QW)
    kc = kc_ref[0]
    kn = kn_ref[...]
    sc = _dot_nt(q, kc)
    sn = _dot_nt(q, kn)
    if has_bias:
        sc = sc + jnp.concatenate([bc_ref[...]] * MLA_HEADS, axis=0)
        sn = sn + jnp.concatenate([bn_ref[...]] * MLA_HEADS, axis=0)
    m = jnp.maximum(jnp.max(sc, axis=-1, keepdims=True), jnp.max(sn, axis=-1, keepdims=True))
    pc = jnp.exp2(sc - m)
    pn = jnp.exp2(sn - m)
    l = jnp.sum(pc, axis=-1, keepdims=True) + jnp.sum(pn, axis=-1, keepdims=True)
    o = (_dot(pc.astype(BF16), kc[:, :MLA_KV_LORA]) + _dot(pn.astype(BF16), kn[:, :MLA_KV_LORA])) / l
    for h in range(MLA_HEADS):
        o_ref[:, h * MLA_KV_LORA:(h + 1) * MLA_KV_LORA] = o[h * tq:(h + 1) * tq].astype(BF16)


def _mla_sample(q, kmat_new, kmat_cache, batch, seq, past_len):
    n = batch * seq
    q_pos = past_len + np.arange(seq)
    k_pos = np.arange(past_len + seq)
    mask = (k_pos[None, :] // CHUNK) <= (q_pos[:, None] // CHUNK)
    has_bias = not bool(mask.all())
    in_specs = [pl.BlockSpec((MLA_HEADS, seq, QW), lambda b: (0, b, 0)),
                pl.BlockSpec((1, past_len, QW), lambda b: (b, 0, 0)),
                pl.BlockSpec((seq, QW), lambda b: (b, 0))]
    args = [q, kmat_cache, kmat_new]
    if has_bias:
        bias = np.where(mask, 0.0, NEG_BIG).astype(np.float32)
        in_specs += [_const_spec((seq, past_len)), _const_spec((seq, seq))]
        args += [jnp.asarray(bias[:, :past_len]), jnp.asarray(bias[:, past_len:])]
    return pl.pallas_call(
        functools.partial(_mla_sample_body, tq=seq, has_bias=has_bias),
        grid=(batch,),
        in_specs=in_specs,
        out_specs=pl.BlockSpec((seq, MLA_HEADS * MLA_KV_LORA), lambda b: (b, 0)),
        out_shape=jax.ShapeDtypeStruct((n, MLA_HEADS * MLA_KV_LORA), BF16),
        compiler_params=_params(("parallel",)),
        name="mla_sample",
    )(*args)


def _mlstm_body(qk_ref, v_ref, og_ref, gates_ref, cw_ref, cb_ref, gb_ref, c0_ref, n0_ref, m0_ref, conv0_ref,
                out_ref, c1_ref, n1_ref, m1_ref, conv1_ref,
                xbuf, qk_s, g_s, cbd, n_s, m_s, *, tb, chunk):
    j = pl.program_id(1)
    H, DK, DV = ML_HEADS, ML_DK, ML_DV
    HD = H * DK
    L = chunk
    pad = 8

    @pl.when(j == 0)
    def _init():
        xbuf[0:pad, :] = conv0_ref[0]
        cbd[...] = jnp.zeros_like(cbd)
        for h in range(H):
            cbd[h * DK:(h + 1) * DK, h * DV:(h + 1) * DV] = c0_ref[0, h]
        n_s[...] = n0_ref[0]
        m_s[...] = m0_ref[0]

    _mlstm_conv(qk_ref, cw_ref, cb_ref, xbuf, qk_s, tb, pad)
    gl = gates_ref[...] + gb_ref[...]
    lane_g = lax.broadcasted_iota(jnp.int32, (1, LANES), 1)
    g_s[...] = jnp.where(lane_g < H, gl, jnp.minimum(gl, 0.0) - jnp.log1p(jnp.exp(-jnp.abs(gl))))

    ri = lax.broadcasted_iota(jnp.int32, (L, L), 0)
    ci = lax.broadcasted_iota(jnp.int32, (L, L), 1)
    tri = (ci <= ri).astype(BF16)
    sel = (lax.broadcasted_iota(jnp.int32, (8, LANES), 0) == lax.broadcasted_iota(jnp.int32, (8, LANES), 1)).astype(BF16)
    row_l = lax.broadcasted_iota(jnp.int32, (L, H * L), 0)
    col_l = lax.broadcasted_iota(jnp.int32, (L, H * L), 1)
    causal = (col_l % L) <= row_l
    grp_d = _lane_group(HD, DK)
    rgrp = lax.broadcasted_iota(jnp.int32, (HD, 1), 0) // DK
    bd_mask = rgrp == _lane_group(H * DV, DV)

    def chunk_step(c, carry):
        r0 = pl.multiple_of(c * L, L)
        q = qk_s[pl.ds(r0, L), 0:HD]
        k = qk_s[pl.ds(r0, L), HD:2 * HD]
        v = v_ref[pl.ds(r0, L), :]
        g = g_s[pl.ds(r0, L), :]
        cs = _sel_dot(tri, g)
        x = jnp.where(lane_g < H, g, cs)
        xt = _sel_dot_nt(sel, x)
        li_c = [x[:, h:h + 1] for h in range(H)]
        b_c = [x[:, H + h:H + h + 1] for h in range(H)]
        m_prev = [m_s[:, h:h + 1] for h in range(H)]
        row_e = jnp.concatenate([xt[h:h + 1, :] - xt[H + h:H + h + 1, :] for h in range(H)], axis=1)
        logw = jnp.where(causal, _expand_cols(b_c, L) + row_e, -jnp.inf)
        m_intra = _seg_max(logw, L, H)
        log_inter = [b_c[h] + m_prev[h] for h in range(H)]
        m_t = [jnp.maximum(log_inter[h], m_intra[h]) for h in range(H)]
        w = jnp.exp(logw - _expand_cols(m_t, L))
        qb = q.astype(BF16)
        kexp = jnp.concatenate([jnp.where(grp_d == h, k, 0.0) for h in range(H)], axis=0).astype(BF16)
        vexp = jnp.concatenate([jnp.where(grp_d == h, v, 0.0) for h in range(H)], axis=0).astype(BF16)
        p = _dot_nt(qb, kexp) * w
        den_intra = _seg_sum(p, L, H)
        num = _dot(p.astype(BF16), vexp)
        w_inter = [jnp.exp(log_inter[h] - m_t[h]) for h in range(H)]
        num = num + _expand_cols(w_inter, DV) * _dot(qb, cbd[...].astype(BF16))
        qn = _seg_sum(q * n_s[...], DK, H)
        inv = [1.0 / jnp.maximum(jnp.abs(den_intra[h] + w_inter[h] * qn[h]), jnp.exp(-m_t[h])) for h in range(H)]
        hout = num * _expand_cols(inv, DV)
        og = og_ref[pl.ds(r0, L), :]
        out_ref[pl.ds(r0, L), :] = (jax.nn.sigmoid(og) * hout).astype(BF16)
        m_new = [m_t[h][L - 1:L, :] for h in range(H)]
        b_last = [b_c[h][L - 1:L, :] for h in range(H)]
        w_s = [jnp.exp(b_last[h] - b_c[h] + li_c[h] - m_new[h]) for h in range(H)]
        decay = [jnp.exp(b_last[h] + m_prev[h] - m_new[h]) for h in range(H)]
        kw = k * _expand_cols(w_s, DK)
        upd = _dot_tn(kw.astype(BF16), v.astype(BF16))
        dcol = decay[H - 1]
        for h in range(H - 2, -1, -1):
            dcol = jnp.where(rgrp == h, decay[h], dcol)
        cbd[...] = dcol * cbd[...] + jnp.where(bd_mask, upd, 0.0)
        n_s[...] = _expand_cols(decay, DK) * n_s[...] + jnp.sum(kw, axis=0, keepdims=True)
        m_row = m_s[...]
        for h in range(H):
            m_row = jnp.where(lane_g == h, m_new[h], m_row)
        m_s[...] = m_row
        return carry

    lax.fori_loop(0, tb // L, chunk_step, 0)

    @pl.when(j == pl.num_programs(1) - 1)
    def _fin():
        for h in range(H):
            c1_ref[0, h] = cbd[h * DK:(h + 1) * DK, h * DV:(h + 1) * DV]
        n1_ref[0] = n_s[...]
        m1_ref[0] = m_s[...]
        conv1_ref[0] = xbuf[0:pad, :]


def _mlstm_conv(qk_ref, cw_ref, cb_ref, xbuf, qk_s, tb, pad):
    HD = ML_HEADS * ML_DK
    xbuf[pad:pad + tb, :] = qk_ref[...]
    conv = cb_ref[...] + cw_ref[ML_CONV - 1:ML_CONV, :] * xbuf[pad:pad + tb, :]
    for t in range(1, ML_CONV):
        conv = conv + cw_ref[ML_CONV - 1 - t:ML_CONV - t, :] * xbuf[pad - t:pad - t + tb, :]
    act = conv * jax.nn.sigmoid(conv)
    lane = lax.broadcasted_iota(jnp.int32, (1, 2 * HD), 1)
    qk_s[...] = jnp.where(lane < HD, act, act * (ML_DK ** -0.5))
    xbuf[0:pad, :] = xbuf[tb:tb + pad, :]


def _mlstm_fast_body(qk_ref, v_ref, og_ref, gates_ref, cw_ref, cb_ref, gbc_ref, utri_ref, ecols_ref,
                     c0_ref, n0_ref, mc0_ref, mr0_ref, conv0_ref,
                     out_ref, c1_ref, n1_ref, m1_ref, conv1_ref,
                     xbuf, qk_s, y_s, u_s, cb_s, ks_s, dec_s, nr_s, num_s, den_s, wi_s, cbd, n_s, m_c, mrow_s, *, tb):
    j = pl.program_id(1)
    H, DK, DV = ML_HEADS, ML_DK, ML_DV
    HD = H * DK
    L = DK
    nc = tb // L
    pad = 8

    @pl.when(j == 0)
    def _init():
        xbuf[0:pad, :] = conv0_ref[0]
        cbd[...] = jnp.zeros_like(cbd)
        for h in range(H):
            cbd[h * DK:(h + 1) * DK, h * DV:(h + 1) * DV] = c0_ref[0, h]
        n_s[...] = n0_ref[0]
        m_c[...] = mc0_ref[0]
        mrow_s[...] = mr0_ref[0]

    g8 = gates_ref[...].T[0:8, :] + gbc_ref[...]
    lsig = jnp.minimum(g8, 0.0) - jnp.log1p(jnp.exp(-jnp.abs(g8)))
    lf8 = pltpu.roll(lsig, 8 - H, axis=0)
    fh, fm, fl = _split3(lf8)
    b8 = _dot(fh, utri_ref[...]) + _dot(fm, utri_ref[...]) + _dot(fl, utri_ref[...])
    a8 = g8 - b8
    lane_t = lax.broadcasted_iota(jnp.int32, (1, tb), 1)
    cmax = a8
    sh = 1
    while sh < tb:
        cmax = jnp.maximum(cmax, jnp.where(lane_t >= sh, pltpu.roll(cmax, sh, axis=1), -jnp.inf))
        sh *= 2
    mp8 = jnp.maximum(m_c[:, 0:1], cmax)
    mt8 = b8 + mp8
    valid = lax.broadcasted_iota(jnp.int32, (8, 1), 0) < H
    x32 = jnp.concatenate([jnp.where(valid, x, 0.0) for x in (mp8, a8, jnp.exp(-mt8), mt8)], axis=0)
    xh, xm, xl = _split3(x32)
    ec = ecols_ref[...]
    y_s[...] = _dot_tn(xh, ec) + _dot_tn(xm, ec) + _dot_tn(xl, ec)
    m_c[...] = jnp.broadcast_to(mt8[:, tb - 1:tb], m_c.shape)

    _mlstm_conv(qk_ref, cw_ref, cb_ref, xbuf, qk_s, tb, pad)

    grp = _lane_group(HD, DK)
    rgrp = lax.broadcasted_iota(jnp.int32, (HD, 1), 0) // DK
    bd_mask = rgrp == grp
    seg_b = bd_mask.astype(BF16)
    row_l = lax.broadcasted_iota(jnp.int32, (L, HD), 0)
    col_l = lax.broadcasted_iota(jnp.int32, (L, HD), 1) % L
    causal = col_l <= row_l
    diag = col_l == row_l

    for c in range(nc):
        r0 = c * L
        q = qk_s[r0:r0 + L, 0:HD]
        k = qk_s[r0:r0 + L, HD:2 * HD]
        vb = v_ref[r0:r0 + L, :].astype(BF16)
        m_e = y_s[r0:r0 + L, 0:HD]
        a_e = y_s[r0:r0 + L, HD:2 * HD]
        m_prev = mrow_s[...] if c == 0 else y_s[r0 - 1:r0, 0:HD]
        m_last = m_e[L - 1:L, :]
        row_a = jnp.sum(jnp.where(diag, a_e, 0.0), axis=0, keepdims=True)
        d = jnp.exp(jnp.where(causal, row_a - m_e, -jnp.inf))
        qb = q.astype(BF16)
        kb = k.astype(BF16)
        kexp = jnp.concatenate([jnp.where(grp == h, kb, jnp.zeros_like(kb)) for h in range(H)], axis=0)
        vexp = jnp.concatenate([jnp.where(grp == h, vb, jnp.zeros_like(vb)) for h in range(H)], axis=0)
        p = _dot_nt(qb, kexp) * d
        p_hi = p.astype(BF16)
        p_lo = (p - p_hi.astype(F32)).astype(BF16)
        num_s[r0:r0 + L, :] = _dot(p_hi, vexp)
        den_s[r0:r0 + L, :] = _dot(p_hi, seg_b) + _dot(p_lo, seg_b)
        wi_s[r0:r0 + L, :] = jnp.exp(m_prev - m_e)
        kw = k * jnp.exp(a_e - m_last)
        u_s[c] = jnp.where(bd_mask, _dot_tn(kw.astype(BF16), vb), 0.0)
        ks_s[c:c + 1, :] = jnp.sum(kw, axis=0, keepdims=True)
        dec_s[c:c + 1, :] = jnp.exp(m_prev - m_last)

    for c in range(nc):
        cb_s[c] = cbd[...].astype(BF16)
        nr_s[c:c + 1, :] = n_s[...]
        dec = dec_s[c:c + 1, :]
        cbd[...] = dec * cbd[...] + u_s[c]
        n_s[...] = dec * n_s[...] + ks_s[c:c + 1, :]

    for c in range(nc):
        r0 = c * L
        q = qk_s[r0:r0 + L, 0:HD]
        qn = q * nr_s[c:c + 1, :]
        qn_hi = qn.astype(BF16)
        qn_lo = (qn - qn_hi.astype(F32)).astype(BF16)
        w_inter = wi_s[r0:r0 + L, :]
        num = num_s[r0:r0 + L, :] + w_inter * _dot(q.astype(BF16), cb_s[c])
        den = den_s[r0:r0 + L, :] + w_inter * (_dot(qn_hi, seg_b) + _dot(qn_lo, seg_b))
        hout = num / jnp.maximum(jnp.abs(den), y_s[r0:r0 + L, 2 * HD:3 * HD])
        og = og_ref[r0:r0 + L, :]
        out_ref[r0:r0 + L, :] = (jax.nn.sigmoid(og) * hout).astype(BF16)

    mrow_s[...] = y_s[tb - 1:tb, 3 * HD:4 * HD]

    @pl.when(j == pl.num_programs(1) - 1)
    def _fin():
        for h in range(H):
            c1_ref[0, h] = cbd[h * DK:(h + 1) * DK, h * DV:(h + 1) * DV]
        n1_ref[0] = n_s[...]
        m1_ref[0] = m_c[...]
        conv1_ref[0] = xbuf[0:pad, :]


def _mlstm_fast(qk, v, og, gates, conv_w, conv_b, gate_bias, c0, n0, m0, conv0, batch, seq, tb):
    n = batch * seq
    H, DK, DV = ML_HEADS, ML_DK, ML_DV
    HD = H * DK
    nb = seq // tb
    nc = tb // DK
    pad = 8
    n0p = n0.reshape(batch, 1, HD)
    mc0 = jnp.broadcast_to(jnp.pad(m0, ((0, 0), (0, 8 - H)))[:, :, None], (batch, 8, LANES))
    mr0 = jnp.repeat(m0, DV, axis=1).reshape(batch, 1, HD)
    conv0p = jnp.pad(conv0, ((0, 0), (pad - (ML_CONV - 1), 0), (0, 0)))
    gbc = jnp.concatenate([gate_bias[0], gate_bias[1]])[:, None]
    idx = np.arange(tb)
    utri = jnp.asarray(idx[:, None] <= idx[None, :], BF16)
    ecols_np = np.zeros((32, 4 * HD), np.float32)
    for kq in range(4):
        for h in range(H):
            ecols_np[8 * kq + h, kq * HD + h * DV:kq * HD + (h + 1) * DV] = 1.0
    ecols = jnp.asarray(ecols_np, BF16)
    row = lambda w: pl.BlockSpec((tb, w), lambda b, j: (b * nb + j, 0))
    per_b = lambda shape: pl.BlockSpec((1,) + shape, lambda b, j: (b,) + (0,) * len(shape))
    f32 = lambda *shape: pltpu.VMEM(shape, F32)
    out, c1, n1, m1, conv1 = pl.pallas_call(
        functools.partial(_mlstm_fast_body, tb=tb),
        grid=(batch, nb),
        in_specs=[row(2 * HD), row(H * DV), row(H * DV), row(LANES),
                  _const_spec((ML_CONV, 2 * HD)), _const_spec((1, 2 * HD)), _const_spec((2 * H, 1)),
                  _const_spec((tb, tb)), _const_spec((32, 4 * HD)),
                  per_b((H, DK, DV)), per_b((1, HD)), per_b((8, LANES)), per_b((1, HD)), per_b((pad, 2 * HD))],
        out_specs=[row(H * DV), per_b((H, DK, DV)), per_b((1, HD)), per_b((8, LANES)), per_b((pad, 2 * HD))],
        out_shape=[jax.ShapeDtypeStruct((n, H * DV), BF16), jax.ShapeDtypeStruct((batch, H, DK, DV), F32),
                   jax.ShapeDtypeStruct((batch, 1, HD), F32), jax.ShapeDtypeStruct((batch, 8, LANES), F32),
                   jax.ShapeDtypeStruct((batch, pad, 2 * HD), F32)],
        scratch_shapes=[f32(tb + pad, 2 * HD), f32(tb, 2 * HD), f32(tb, 4 * HD), f32(nc, HD, H * DV),
                        pltpu.VMEM((nc, HD, H * DV), BF16), f32(nc, HD), f32(nc, HD), f32(nc, HD),
                        f32(tb, HD), f32(tb, HD), f32(tb, HD), f32(HD, H * DV), f32(1, HD), f32(8, LANES),
                        f32(1, HD)],
        compiler_params=_params(("parallel", "arbitrary")),
        name="mlstm_blocked",
    )(qk, v, og, gates, conv_w, conv_b, gbc, utri, ecols, c0, n0p, mc0, mr0, conv0p)
    return (out, c1, n1.reshape(batch, H, DK), m1[:, :H, 0], conv1[:, pad - (ML_CONV - 1):, :])


def _mlstm(qk, v, og, gates, conv_w, conv_b, gate_bias, c0, n0, m0, conv0, batch, seq):
    if ML_DK == ML_DV == CHUNK and seq % 512 == 0:
        return _mlstm_fast(qk, v, og, gates, conv_w, conv_b, gate_bias, c0, n0, m0, conv0, batch, seq, 512)
    gb = gate_bias
    gbias = jnp.concatenate([gb[0], gb[1], jnp.zeros((LANES - 2 * ML_HEADS,), F32)])[None, :]
    n = batch * seq
    H, DK, DV = ML_HEADS, ML_DK, ML_DV
    chunk = min(CHUNK, seq)
    assert seq % chunk == 0 and seq >= ML_CONV - 1 and chunk % 8 == 0
    tb = 512 if seq % 512 == 0 else chunk
    nb = seq // tb
    pad = 8
    n0p = n0.reshape(batch, 1, H * DK)
    m0p = jnp.pad(m0.reshape(batch, 1, H), ((0, 0), (0, 0), (0, LANES - H)))
    conv0p = jnp.pad(conv0, ((0, 0), (pad - (ML_CONV - 1), 0), (0, 0)))
    row = lambda w: pl.BlockSpec((tb, w), lambda b, j: (b * nb + j, 0))
    per_b = lambda shape: pl.BlockSpec((1,) + shape, lambda b, j: (b,) + (0,) * len(shape))
    out, c1, n1, m1, conv1 = pl.pallas_call(
        functools.partial(_mlstm_body, tb=tb, chunk=chunk),
        grid=(batch, nb),
        in_specs=[row(2 * H * DK), row(H * DV), row(H * DV), row(LANES),
                  _const_spec((ML_CONV, 2 * H * DK)), _const_spec((1, 2 * H * DK)), _const_spec((1, LANES)),
                  per_b((H, DK, DV)), per_b((1, H * DK)), per_b((1, LANES)), per_b((pad, 2 * H * DK))],
        out_specs=[row(H * DV), per_b((H, DK, DV)), per_b((1, H * DK)), per_b((1, LANES)), per_b((pad, 2 * H * DK))],
        out_shape=[jax.ShapeDtypeStruct((n, H * DV), BF16), jax.ShapeDtypeStruct((batch, H, DK, DV), F32),
                   jax.ShapeDtypeStruct((batch, 1, H * DK), F32), jax.ShapeDtypeStruct((batch, 1, LANES), F32),
                   jax.ShapeDtypeStruct((batch, pad, 2 * H * DK), F32)],
        scratch_shapes=[pltpu.VMEM((tb + pad, 2 * H * DK), F32), pltpu.VMEM((tb, 2 * H * DK), F32),
                        pltpu.VMEM((tb, LANES), F32), pltpu.VMEM((H * DK, H * DV), F32),
                        pltpu.VMEM((1, H * DK), F32), pltpu.VMEM((1, LANES), F32)],
        compiler_params=_params(("parallel", "arbitrary")),
        name="mlstm",
    )(qk, v, og, gates, conv_w, conv_b, gbias, c0, n0p, m0p, conv0p)
    return (out, c1, n1.reshape(batch, H, DK), m1[:, 0, :H], conv1[:, pad - (ML_CONV - 1):, :])


def _hgrn_body(hg_ref, lbraw_ref, hnorm_ref, tri_ref, s0_ref, out_ref, s1_ref,
               hq_s, kk_s, f_s, b_s, qe_s, oi_s, u_s, sts_s, dec_s, st, *, tb, sub, layer):
    j = pl.program_id(1)
    H, DK, DV = HG_HEADS, HG_DK, HG_DV
    HD = H * DK
    S = sub

    @pl.when(j == 0)
    def _init():
        st[...] = jnp.zeros_like(st)
        for h in range(H):
            st[h * DV:(h + 1) * DV, h * DK:(h + 1) * DK] = s0_ref[0, h].T

    raw = lbraw_ref[...]
    e = jnp.exp(raw - jnp.max(raw, axis=0, keepdims=True))
    sm = e / jnp.sum(e, axis=0, keepdims=True)
    cum = sm[0:1, :]
    for l in range(1, layer + 1):
        cum = cum + sm[l:l + 1, :]
    lb = cum - sm[0:1, :]

    z = hg_ref[:, HD:2 * HD]
    lsig = jnp.minimum(z, 0.0) - jnp.log1p(jnp.exp(-jnp.abs(z)))
    a1 = jnp.log(lb)
    a2 = jnp.log1p(-lb) + lsig
    delta = a1 - a2
    lf = jnp.where(jnp.isnan(delta), a1 + a2, jnp.maximum(a1, a2) + jnp.log1p(jnp.exp(-jnp.abs(delta))))
    f_s[...] = jnp.exp(lf)
    b_all = _sel_dot(tri_ref[...], lf)
    b_s[...] = b_all
    kk_s[...] = (1.0 - lb) * jax.nn.sigmoid(-z)
    xq = hg_ref[:, 0:HD]
    hq_all = xq * jax.nn.sigmoid(xq)
    hq_s[...] = hq_all
    qe_s[...] = (hq_all * jnp.exp(b_all)).astype(BF16)

    rowi = lax.broadcasted_iota(jnp.int32, (S, 1), 0)
    seg = ((lax.broadcasted_iota(jnp.int32, (HD, 1), 0) // DK) == _lane_group(H * DV, DV))
    seg_b = seg.astype(BF16)
    gnorm = hnorm_ref[...]

    n_sub = tb // S

    for c in range(n_sub):
        r0 = c * S
        hq = hq_s[r0:r0 + S, :]
        kk = kk_s[r0:r0 + S, :]
        f = f_s[r0:r0 + S, :]
        b = b_s[r0:r0 + S, :]
        iv = hg_ref[r0:r0 + S, 2 * HD:2 * HD + H * DV]
        bl = b[S - 1:S, :]
        kd = kk * jnp.exp(bl - b)
        u_s[c] = jnp.where(seg, _dot_tn(iv.astype(BF16), kd.astype(BF16)), 0.0)
        dec_s[c:c + 1, :] = jnp.exp(bl)
        qd = jnp.where(rowi == S - 1, hq, 0.0)
        ws = [None] * S
        ws[S - 1] = (qd * kk[S - 1:S, :]).astype(BF16)
        for s in range(S - 2, -1, -1):
            qd = jnp.where(rowi == s, hq, qd * f[s + 1:s + 2, :])
            ws[s] = (qd * kk[s:s + 1, :]).astype(BF16)
        r = _dot(jnp.concatenate(ws, axis=0), seg_b)
        oi = r[0:S, :] * iv[0:1, :]
        for s in range(1, S):
            oi = oi + r[s * S:(s + 1) * S, :] * iv[s:s + 1, :]
        oi_s[r0:r0 + S, :] = oi

    for c in range(n_sub):
        sts_s[c] = st[...].astype(BF16)
        st[...] = dec_s[c:c + 1, :] * st[...] + u_s[c]

    grp_rows = min(n_sub, 4) * S
    for c0 in range(0, n_sub, grp_rows // S):
        r0 = c0 * S
        o = jnp.concatenate([oi_s[(c0 + i) * S:(c0 + i + 1) * S, :]
                             + _dot_nt(qe_s[(c0 + i) * S:(c0 + i + 1) * S, :], sts_s[c0 + i])
                             for i in range(grp_rows // S)], axis=0)
        gt = hg_ref[r0:r0 + grp_rows, 2 * HD + H * DV:2 * HD + 2 * H * DV]
        ms = _dot((o * o).astype(BF16), seg_b) * (1.0 / DV)
        y = o * lax.rsqrt(ms + EPS) * gnorm * (gt * jax.nn.sigmoid(gt))
        out_ref[r0:r0 + grp_rows, :] = y.astype(BF16)

    @pl.when(j == pl.num_programs(1) - 1)
    def _fin():
        for h in range(H):
            s1_ref[0, h] = st[h * DV:(h + 1) * DV, h * DK:(h + 1) * DK].T


def _hgrn(hg, lb_raw, hnorm, s0, batch, seq, layer):
    n = batch * seq
    H, DK, DV = HG_HEADS, HG_DK, HG_DV
    assert DK == DV
    sub = min(HG_SUB, seq)
    assert seq % sub == 0 and sub % 8 == 0
    tb = 512 if seq % 512 == 0 else sub
    nb = seq // tb
    depth = lb_raw.shape[0]
    n_sub = tb // sub
    idx = np.arange(tb)
    tri = jnp.asarray((idx[:, None] // sub == idx[None, :] // sub) & (idx[None, :] <= idx[:, None]), BF16)
    per_b = pl.BlockSpec((1, H, DK, DV), lambda b, j: (b, 0, 0, 0))
    out, s1 = pl.pallas_call(
        functools.partial(_hgrn_body, tb=tb, sub=sub, layer=layer),
        grid=(batch, nb),
        in_specs=[pl.BlockSpec((tb, 4 * H * DK), lambda b, j: (b * nb + j, 0)), _const_spec((depth, H * DK)),
                  _const_spec((1, H * DV)), _const_spec((tb, tb)), per_b],
        out_specs=[pl.BlockSpec((tb, H * DV), lambda b, j: (b * nb + j, 0)), per_b],
        out_shape=[jax.ShapeDtypeStruct((n, H * DV), BF16), jax.ShapeDtypeStruct((batch, H, DK, DV), F32)],
        scratch_shapes=[pltpu.VMEM((tb, H * DK), F32), pltpu.VMEM((tb, H * DK), F32), pltpu.VMEM((tb, H * DK), F32),
                        pltpu.VMEM((tb, H * DK), F32), pltpu.VMEM((tb, H * DK), BF16),
                        pltpu.VMEM((tb, H * DV), F32), pltpu.VMEM((n_sub, H * DV, H * DK), F32),
                        pltpu.VMEM((n_sub, H * DV, H * DK), BF16), pltpu.VMEM((max(n_sub, 8), H * DK), F32),
                        pltpu.VMEM((H * DV, H * DK), F32)],
        compiler_params=_params(("parallel", "arbitrary")),
        name="hgrn2",
    )(hg, lb_raw, hnorm, tri, s0)
    return out, s1


def _outproj_ffn_body(x_ref, o_ref, ml_ref, hgo_ref, g_ref, w1_ref, w2_ref, w3_ref,
                      gpre_ref, gpost_ref, wg_ref, wu_ref, wo_ref, y_ref, xn_ref, acc_ref, *, fc):
    mix = _dot(o_ref[...], w1_ref[...]) + _dot(ml_ref[...], w2_ref[...]) + _dot(hgo_ref[...], w3_ref[...])
    x1 = x_ref[...] + _rms(mix, g_ref[...])
    y_ref[...] = _ffn_apply(x1, gpre_ref, gpost_ref, wg_ref, wu_ref, wo_ref, xn_ref, acc_ref, fc)


def _outproj_ffn(x, o_lat, ml_out, hg_out, g, w1, w2, w3, g_pre, g_post, w_gate, w_up, w_out):
    n, d = x.shape
    d_ff = w_gate.shape[1]
    tm = _token_tile(n)
    fc = 256 if d_ff % 256 == 0 else d_ff
    row = lambda w: pl.BlockSpec((tm, w), lambda i: (i, 0))
    return pl.pallas_call(
        functools.partial(_outproj_ffn_body, fc=fc),
        grid=(n // tm,),
        in_specs=[row(d), row(o_lat.shape[1]), row(ml_out.shape[1]), row(hg_out.shape[1]), _const_spec((1, d)),
                  _const_spec(w1.shape), _const_spec(w2.shape), _const_spec(w3.shape),
                  _const_spec((1, d)), _const_spec((1, d)), _const_spec((d, d_ff)), _const_spec((d, d_ff)),
                  _const_spec((d_ff, d))],
        out_specs=row(d),
        out_shape=jax.ShapeDtypeStruct((n, d), F32),
        scratch_shapes=[pltpu.VMEM((tm, d), BF16), pltpu.VMEM((tm, d), F32)],
        compiler_params=_params(("parallel",)),
        name="out_proj_ffn",
    )(x, o_lat, ml_out, hg_out, g, w1, w2, w3, g_pre, g_post, w_gate, w_up, w_out)


def _rope_tables(pos):
    inv = ROPE_THETA ** (-jnp.arange(0, MLA_ROPE, 2, dtype=F32) / MLA_ROPE)
    ang = pos.astype(F32)[:, None] * inv[None, :]
    cos, sin = jnp.cos(ang), jnp.sin(ang)
    t = pos.shape[0]
    cos2, sin2 = jnp.concatenate([cos, cos], -1), jnp.concatenate([sin, sin], -1)
    tabq = (MLA_SCALE * LOG2E) * jnp.concatenate([jnp.ones((t, MLA_KV_LORA), F32), cos2, sin2,
                                        jnp.zeros((t, QW - MLA_KV_LORA - 2 * MLA_ROPE), F32)], -1)
    zpad = jnp.zeros((t, LANES - 2 * MLA_ROPE), F32)
    cos_t = jnp.concatenate([cos2, cos2, zpad], -1)
    sin_t = jnp.concatenate([sin2, sin2, zpad], -1)
    return tabq, cos_t, sin_t


def _swap_halves(w):
    half = w.shape[-1] // 2
    return jnp.concatenate([-w[..., half:], w[..., :half]], axis=-1)


def _layer_weights(l, ln_gains, w_ffn_in, w_ffn_out, w_in, w_out, mla_q_norm, mla_kv_norm, mla_w_uq, mla_w_uk,
                   mla_w_uv, ml_conv_w, ml_conv_b, ml_gate_bias, hg_norm):
    d = w_in.shape[1]
    d_ff = w_ffn_out.shape[2]
    sizes = (MLA_Q_LORA, MLA_KV_LORA, MLA_ROPE, 2 * ML_HEADS * ML_DK, ML_HEADS * ML_DV, ML_HEADS * ML_DV, ML_HEADS,
             ML_HEADS, HG_HEADS * HG_DK, HG_HEADS * HG_DK, HG_HEADS * HG_DV, HG_HEADS * HG_DV)
    pts = np.cumsum((0,) + sizes)
    part = [w_in[l][:, pts[i]:pts[i + 1]] for i in range(len(sizes))]
    zeros = lambda n: jnp.zeros((d, n), F32)
    kpe, kpe_sw = part[2], _swap_halves(part[2])
    w_perm = jnp.concatenate(
        [part[0], part[1], kpe, kpe, zeros(LANES - 2 * MLA_ROPE), kpe_sw, kpe_sw, zeros(LANES - 2 * MLA_ROPE),
         part[3], part[4], part[5], part[6], part[7], zeros(LANES - 2 * ML_HEADS), part[8], part[9], part[10],
         part[11]], axis=1).astype(BF16)
    assert w_perm.shape[1] == _IN_COLS
    uq = mla_w_uq[l].reshape(MLA_Q_LORA, MLA_HEADS, MLA_NOPE + MLA_ROPE)
    uq_nope = jnp.transpose(uq[:, :, :MLA_NOPE], (1, 0, 2))
    uk_t = jnp.transpose(mla_w_uk[l], (1, 2, 0))
    w_lat = _bmm(uq_nope, uk_t)
    uq_pe = jnp.transpose(uq[:, :, MLA_NOPE:], (1, 0, 2))
    wq = jnp.concatenate([w_lat, uq_pe, _swap_halves(uq_pe),
                          jnp.zeros((MLA_HEADS, MLA_Q_LORA, QW - MLA_KV_LORA - 2 * MLA_ROPE), F32)], axis=-1)
    wq = jnp.transpose(wq, (1, 0, 2)).reshape(MLA_Q_LORA, MLA_HEADS * QW).astype(BF16)
    n_mla = MLA_HEADS * MLA_V
    uv = jnp.transpose(mla_w_uv[l], (1, 0, 2))
    wo_mla = w_out[l][:n_mla].reshape(MLA_HEADS, MLA_V, d)
    w1 = _bmm(uv, wo_mla).reshape(MLA_HEADS * MLA_KV_LORA, d).astype(BF16)
    n_ml = ML_HEADS * ML_DV
    w2 = w_out[l][n_mla:n_mla + n_ml].astype(BF16)
    w3 = w_out[l][n_mla + n_ml:].astype(BF16)
    ffn = []
    for j in range(2):
        wi = w_ffn_in[l, j]
        ffn.append((wi[:, :d_ff].astype(BF16), wi[:, d_ff:].astype(BF16), w_ffn_out[l, j].astype(BF16)))
    return dict(ln=ln_gains[l], ffn=ffn, w_perm=w_perm, wq=wq, w1=w1, w2=w2, w3=w3,
                q_norm=mla_q_norm[l][None, :], kv_norm=mla_kv_norm[l][None, :], conv_w=ml_conv_w[l],
                conv_b=ml_conv_b[l][None, :], gate_bias=ml_gate_bias[l].astype(F32), hnorm=jnp.tile(hg_norm[l], HG_HEADS)[None, :])


def _tile_rows(t, reps):
    return jnp.tile(t, (reps, 1)) if reps > 1 else t


def _layer(x, lw, layer, lb_raw, batch, seq, pos, past):
    n, d = x.shape
    g = lw['ln']
    gain = lambda i: g[i][None, :]
    x = _ffn(x, gain(0), gain(1), *lw['ffn'][0])
    tabq, cos_t, sin_t = _rope_tables(pos)
    tm = _token_tile(n)
    reps = tm // seq if seq < tm else 1
    tabs = [_tile_rows(t, reps) for t in (tabq, cos_t, sin_t)]
    q, kmat, ckv, kpe, mlqk, mlv, mlo, gates, hg = _inproj(x, gain(2), lw['w_perm'], lw['q_norm'], lw['kv_norm'],
                                                          lw['wq'], *tabs, seq)
    if past is None:
        o_lat = _mla_prompt(q, kmat, batch, seq)
        c0 = jnp.zeros((batch, ML_HEADS, ML_DK, ML_DV), F32)
        n0 = jnp.zeros((batch, ML_HEADS, ML_DK), F32)
        m0 = jnp.zeros((batch, ML_HEADS), F32)
        conv0 = jnp.zeros((batch, ML_CONV - 1, 2 * ML_HEADS * ML_DK), F32)
        s0 = jnp.zeros((batch, HG_HEADS, HG_DK, HG_DV), F32)
    else:
        cache_ckv, cache_kpe, c0, n0, m0, conv0, s0 = past
        past_len = cache_ckv.shape[1]
        kmat_cache = jnp.concatenate(
            [cache_ckv, cache_kpe, cache_kpe,
             jnp.ones((batch, past_len, 1), F32),
             jnp.zeros((batch, past_len, QW - _K_ONE - 1), F32)], axis=-1).astype(BF16)
        o_lat = _mla_sample(q, kmat, kmat_cache, batch, seq, past_len)
    ml_out, c1, n1, m1, conv1 = _mlstm(mlqk, mlv, mlo, gates, lw['conv_w'], lw['conv_b'], lw['gate_bias'],
                                       c0, n0, m0, conv0, batch, seq)
    hg_out, s1 = _hgrn(hg, lb_raw, lw['hnorm'], s0, batch, seq, layer)
    x = _outproj_ffn(x, o_lat, ml_out, hg_out, gain(3), lw['w1'], lw['w2'], lw['w3'],
                     gain(4), gain(5), *lw['ffn'][1])
    state = (ckv.reshape(batch, seq, MLA_KV_LORA), kpe.reshape(batch, seq, MLA_ROPE), c1, n1, m1, conv1, s1)
    return x, state


def kernel(x_prompt, x_sample, cache_ckv, cache_kpe, state_mlstm_c, state_mlstm_n, state_mlstm_m, state_mlstm_conv, state_hgrn, ln_gains, w_ffn_in, w_ffn_out, w_in, w_out, mla_q_norm, mla_kv_norm, mla_w_uq, mla_w_uk, mla_w_uv, ml_conv_w, ml_conv_b, ml_gate_bias, hg_lb_raw, hg_norm):
    depth = w_in.shape[0]
    bp, tp, d = x_prompt.shape
    bs, ts, _ = x_sample.shape
    past_len = cache_ckv.shape[2]
    pos_p = jnp.arange(tp)
    pos_s = past_len + jnp.arange(ts)
    yp = x_prompt.reshape(bp * tp, d)
    ys = x_sample.reshape(bs * ts, d)
    lb_raw = hg_lb_raw.astype(F32)
    p_states, s_states = [], []
    for l in range(depth):
        lw = _layer_weights(l, ln_gains, w_ffn_in, w_ffn_out, w_in, w_out, mla_q_norm, mla_kv_norm, mla_w_uq,
                            mla_w_uk, mla_w_uv, ml_conv_w, ml_conv_b, ml_gate_bias, hg_norm)
        yp, st_p = _layer(yp, lw, l, lb_raw, bp, tp, pos_p, None)
        past = (cache_ckv[l], cache_kpe[l], state_mlstm_c[l], state_mlstm_n[l], state_mlstm_m[l],
                state_mlstm_conv[l], state_hgrn[l])
        ys, st_s = _layer(ys, lw, l, lb_raw, bs, ts, pos_s, past)
        p_states.append(st_p)
        s_states.append(st_s)
    p_out = [jnp.stack([st[i] for st in p_states]) for i in range(7)]
    s_out = [jnp.stack([st[i] for st in s_states]) for i in range(7)]
    return (yp.reshape(bp, tp, d), ys.reshape(bs, ts, d), *p_out, *s_out)
```

```python
import functools

import numpy as np
import jax
import jax.numpy as jnp
from jax import lax
from jax.experimental import pallas as pl
from jax.experimental.pallas import tpu as pltpu

F32 = jnp.float32
BF16 = jnp.bfloat16

CHUNK = 64
FFN_RES = 0.5
EPS = 1e-6
MLA_HEADS = 8
MLA_NOPE = 64
MLA_ROPE = 32
MLA_V = 64
MLA_Q_LORA = 256
MLA_KV_LORA = 128
ROPE_THETA = 10000.0
MLA_SCALE = (MLA_NOPE + MLA_ROPE) ** -0.5
ML_HEADS = 4
ML_DK = 64
ML_DV = 64
ML_CONV = 4
HG_HEADS = 4
HG_DK = 64
HG_DV = 64

LANES = 128
VMEM_LIMIT = 56 * 1024 * 1024
QW = 2 * LANES
_K_ONE = MLA_KV_LORA + 2 * MLA_ROPE
LOG2E = 1.4426950408889634
HG_SUB = 16
NEG_BIG = -1e30


def _rms(x, g):
    return x * lax.rsqrt(jnp.mean(x * x, axis=-1, keepdims=True) + EPS) * g


def _dot(a, b):
    return jnp.dot(a, b, preferred_element_type=F32)


def _dot_nt(a, b):
    return lax.dot_general(a, b, (((1,), (1,)), ((), ())), preferred_element_type=F32)


def _dot_tn(a, b):
    return lax.dot_general(a, b, (((0,), (0,)), ((), ())), preferred_element_type=F32)


def _split3(x):
    hi = x.astype(BF16)
    r = x - hi.astype(F32)
    mid = r.astype(BF16)
    lo = (r - mid.astype(F32)).astype(BF16)
    return hi, mid, lo


def _sel_dot(sel, x):
    hi, mid, lo = _split3(x)
    return _dot(sel, hi) + _dot(sel, mid) + _dot(sel, lo)


def _sel_dot_nt(sel, x):
    hi, mid, lo = _split3(x)
    return _dot_nt(sel, hi) + _dot_nt(sel, mid) + _dot_nt(sel, lo)


def _lane_group(n, g):
    return lax.broadcasted_iota(jnp.int32, (1, n), 1) // g


def _expand_cols(cols, g):
    grp = _lane_group(len(cols) * g, g)
    out = cols[-1]
    for h in range(len(cols) - 2, -1, -1):
        out = jnp.where(grp == h, cols[h], out)
    return out


def _seg_sum(x, g, heads):
    grp = _lane_group(heads * g, g)
    return [jnp.sum(jnp.where(grp == h, x, 0.0), axis=-1, keepdims=True) for h in range(heads)]


def _seg_max(x, g, heads):
    grp = _lane_group(heads * g, g)
    return [jnp.max(jnp.where(grp == h, x, -jnp.inf), axis=-1, keepdims=True) for h in range(heads)]


def _const_spec(shape):
    nd = len(shape)
    return pl.BlockSpec(shape, lambda *_: (0,) * nd, pipeline_mode=pl.Buffered(1))


def _params(sem):
    return pltpu.CompilerParams(dimension_semantics=sem, vmem_limit_bytes=VMEM_LIMIT)


def _token_tile(n):
    for t in (512, 256, 128, 64, 32, 16, 8):
        if n % t == 0:
            return t
    raise ValueError(f"token count {n} must be a multiple of 8")


def _bmm_body(a_ref, b_ref, o_ref):
    o_ref[0] = jnp.dot(a_ref[0], b_ref[0], preferred_element_type=F32, precision=lax.Precision.HIGHEST)


def _bmm(a, b):
    h, m, k = a.shape
    n = b.shape[2]
    return pl.pallas_call(
        _bmm_body,
        grid=(h,),
        in_specs=[pl.BlockSpec((1, m, k), lambda i: (i, 0, 0)), pl.BlockSpec((1, k, n), lambda i: (i, 0, 0))],
        out_specs=pl.BlockSpec((1, m, n), lambda i: (i, 0, 0)),
        out_shape=jax.ShapeDtypeStruct((h, m, n), F32),
        compiler_params=_params(("parallel",)),
        name="weight_fold",
    )(a, b)


def _ffn_apply(x, gpre_ref, gpost_ref, wg_ref, wu_ref, wo_ref, xn_ref, acc_ref, fc):
    xn_ref[...] = _rms(x, gpre_ref[...]).astype(BF16)
    d_ff = wg_ref.shape[1]
    for c in range(d_ff // fc):
        xn = xn_ref[...]
        sl = slice(c * fc, (c + 1) * fc)
        gate = _dot(xn, wg_ref[:, sl])
        up = _dot(xn, wu_ref[:, sl])
        act = (gate * jax.nn.sigmoid(gate) * up).astype(BF16)
        part = _dot(act, wo_ref[sl, :])
        if c == 0:
            acc_ref[...] = part
        else:
            acc_ref[...] += part
    return x + FFN_RES * _rms(acc_ref[...], gpost_ref[...])


def _ffn_body(x_ref, gpre_ref, gpost_ref, wg_ref, wu_ref, wo_ref, o_ref, xn_ref, acc_ref, *, fc):
    o_ref[...] = _ffn_apply(x_ref[...], gpre_ref, gpost_ref, wg_ref, wu_ref, wo_ref, xn_ref, acc_ref, fc)


def _ffn(x, g_pre, g_post, w_gate, w_up, w_out):
    n, d = x.shape
    d_ff = w_gate.shape[1]
    tm = _token_tile(n)
    fc = 256 if d_ff % 256 == 0 else d_ff
    row = pl.BlockSpec((tm, d), lambda i: (i, 0))
    return pl.pallas_call(
        functools.partial(_ffn_body, fc=fc),
        grid=(n // tm,),
        in_specs=[row, _const_spec((1, d)), _const_spec((1, d)), _const_spec((d, d_ff)), _const_spec((d, d_ff)),
                  _const_spec((d_ff, d))],
        out_specs=row,
        out_shape=jax.ShapeDtypeStruct((n, d), F32),
        scratch_shapes=[pltpu.VMEM((tm, d), BF16), pltpu.VMEM((tm, d), F32)],
        compiler_params=_params(("parallel",)),
        name="ffn",
    )(x, g_pre, g_post, w_gate, w_up, w_out)


_C_CQ = (0, 256)
_C_CKV = (256, 384)
_C_ROPE_A = (384, 512)
_C_ROPE_B = (512, 640)
_C_MLQK = (640, 1152)
_C_MLV = (1152, 1408)
_C_MLO = (1408, 1664)
_C_GATES = (1664, 1792)
_C_HG = (1792, 2816)
_IN_COLS = 2816


def _inproj_body(x_ref, g_ref, w_ref, qn_ref, kvn_ref, wq_ref, tabq_ref, cos_ref, sin_ref, *rest):
    q_ref, kmat_ref, ckv_ref, kpe_ref, mlqk_ref, mlv_ref, mlo_ref, gates_ref, hg_ref, hn_ref = rest[-10:]
    hn_ref[...] = _rms(x_ref[...], g_ref[...]).astype(BF16)

    def proj(cols):
        return _dot(hn_ref[...], w_ref[:, cols[0]:cols[1]])

    cqn = _rms(proj(_C_CQ), qn_ref[...]).astype(BF16)
    tab = tabq_ref[...]
    for h in range(MLA_HEADS):
        q_ref[h] = (_dot(cqn, wq_ref[:, h * QW:(h + 1) * QW]) * tab).astype(BF16)
    ckv = _rms(proj(_C_CKV), kvn_ref[...])
    ckv_ref[0] = ckv
    rot = proj(_C_ROPE_A) * cos_ref[...] + proj(_C_ROPE_B) * sin_ref[...]
    kpe_ref[0] = rot[:, :MLA_ROPE]
    kmat_ref[:, :MLA_KV_LORA] = ckv.astype(BF16)
    lane = lax.broadcasted_iota(jnp.int32, (1, LANES), 1)
    kmat_ref[:, MLA_KV_LORA:] = jnp.where(lane == _K_ONE - MLA_KV_LORA, 1.0, rot).astype(BF16)
    mlqk_ref[...] = proj(_C_MLQK)
    mlv_ref[...] = proj(_C_MLV)
    mlo_ref[...] = proj(_C_MLO)
    gates_ref[...] = proj(_C_GATES)
    hg_ref[...] = proj(_C_HG)


def _inproj(x, g, w_perm, q_norm, kv_norm, wq, tabq, cos_t, sin_t, seq, layer, depth, stacks):
    n, d = x.shape
    tm = _token_tile(n)
    if tm <= seq:
        assert seq % tm == 0
        nt = seq // tm
        tab_map = lambda i: (i % nt, 0)
    else:
        assert tm % seq == 0 and tabq.shape[0] == tm
        tab_map = lambda i: (0, 0)
    row = lambda w: pl.BlockSpec((tm, w), lambda i: (i, 0))
    tab = lambda w: pl.BlockSpec((tm, w), tab_map)
    outs = [
        (jax.ShapeDtypeStruct((MLA_HEADS, n, QW), BF16), pl.BlockSpec((MLA_HEADS, tm, QW), lambda i: (0, i, 0))),
        (jax.ShapeDtypeStruct((n, QW), BF16), row(QW)),
        (jax.ShapeDtypeStruct((depth, n, MLA_KV_LORA), F32),
         pl.BlockSpec((1, tm, MLA_KV_LORA), lambda i: (layer, i, 0))),
        (jax.ShapeDtypeStruct((depth, n, MLA_ROPE), F32), pl.BlockSpec((1, tm, MLA_ROPE), lambda i: (layer, i, 0))),
        (jax.ShapeDtypeStruct((n, 512), F32), row(512)),
        (jax.ShapeDtypeStruct((n, 256), F32), row(256)),
        (jax.ShapeDtypeStruct((n, 256), F32), row(256)),
        (jax.ShapeDtypeStruct((n, LANES), F32), row(LANES)),
        (jax.ShapeDtypeStruct((n, 1024), F32), row(1024)),
    ]
    in_specs = [row(d), _const_spec((1, d)), _const_spec(w_perm.shape), _const_spec((1, MLA_Q_LORA)),
                _const_spec((1, MLA_KV_LORA)), _const_spec(wq.shape), tab(QW), tab(LANES), tab(LANES)]
    args = [x, g, w_perm, q_norm, kv_norm, wq, tabq, cos_t, sin_t]
    aliases = {}
    if stacks is not None:
        aliases = {len(args): 2, len(args) + 1: 3}
        in_specs += [pl.BlockSpec(memory_space=pl.ANY)] * 2
        args += list(stacks)
    return pl.pallas_call(
        _inproj_body,
        grid=(n // tm,),
        in_specs=in_specs,
        out_specs=[o[1] for o in outs],
        out_shape=[o[0] for o in outs],
        scratch_shapes=[pltpu.VMEM((tm, d), BF16)],
        input_output_aliases=aliases,
        compiler_params=_params(("parallel",)),
        name="in_proj",
    )(*args)


def _mla_prompt_body(q_ref, k_ref, o_ref, acc_ref, s_ref, *, tq, tk, nsub):
    i = pl.program_id(1)

    def keys(j):
        return k_ref[0, pl.ds(pl.multiple_of(j * tk, tk), tk), :]

    def consume(k, m_prev, nxt, shift):
        m_out = []
        for h in range(MLA_HEADS):
            s = s_ref[h]
            if shift is not None:
                r = lax.broadcasted_iota(jnp.int32, (tq, tk), 0)
                c = lax.broadcasted_iota(jnp.int32, (tq, tk), 1)
                s = jnp.where((c // CHUNK) - (r // CHUNK) <= shift, s, NEG_BIG)
            m_new = jnp.maximum(m_prev[h], jnp.max(s, axis=-1, keepdims=True))
            alpha = jnp.exp2(m_prev[h] - m_new)
            p = jnp.exp2(s - m_new).astype(BF16)
            acc_ref[h] = alpha * acc_ref[h] + _dot(p, k)
            if nxt is not None:
                s_ref[h] = _dot_nt(q_ref[h, nxt[0] * tq:(nxt[0] + 1) * tq, :], nxt[1])
            m_out.append(m_new)
        return tuple(m_out)

    acc_ref[...] = jnp.zeros_like(acc_ref)
    k0 = keys(0)
    for h in range(MLA_HEADS):
        s_ref[h] = _dot_nt(q_ref[h, 0:tq, :], k0)
    m0 = tuple(jnp.full((tq, 1), NEG_BIG, F32) for _ in range(MLA_HEADS))
    for u in range(nsub):
        start = (i * nsub + u) * tq
        j_last = start // tk
        m1 = lax.fori_loop(0, j_last, lambda j, m: consume(keys(j), m, (u, keys(j + 1)), None), m0)
        consume(keys(j_last), m1, (u + 1, keys(0)) if u + 1 < nsub else None, (start - j_last * tk) // CHUNK)
        for h in range(MLA_HEADS):
            acc = acc_ref[h]
            o = acc[:, :MLA_KV_LORA] * (1.0 / acc[:, _K_ONE:_K_ONE + 1])
            o_ref[u * tq:(u + 1) * tq, h * MLA_KV_LORA:(h + 1) * MLA_KV_LORA] = o.astype(BF16)


def _mla_prompt(q, kmat, batch, seq):
    n = batch * seq
    tq = 256 if seq % 256 == 0 else seq
    tk = 2 * tq if seq % (2 * tq) == 0 else tq
    assert tq % CHUNK == 0 or tq == seq
    nsub = next(c for c in (8, 4, 2, 1) if seq % (c * tq) == 0)
    nq = seq // (nsub * tq)
    tqs = nsub * tq
    return pl.pallas_call(
        functools.partial(_mla_prompt_body, tq=tq, tk=tk, nsub=nsub),
        grid=(batch, nq),
        in_specs=[pl.BlockSpec((MLA_HEADS, tqs, QW), lambda b, i: (0, b * nq + i, 0)),
                  pl.BlockSpec((1, seq, QW), lambda b, i: (b, 0, 0))],
        out_specs=pl.BlockSpec((tqs, MLA_HEADS * MLA_KV_LORA), lambda b, i: (b * nq + i, 0)),
        out_shape=jax.ShapeDtypeStruct((n, MLA_HEADS * MLA_KV_LORA), BF16),
        scratch_shapes=[pltpu.VMEM((MLA_HEADS, tq, QW), F32), pltpu.VMEM((MLA_HEADS, tq, tk), F32)],
        compiler_params=_params(("parallel", "arbitrary")),
        name="mla_prompt",
    )(q, kmat.reshape(batch, seq, QW))


def _mla_sample_body(*refs, tq, has_bias):
    if has_bias:
        q_ref, kc_ref, kn_ref, bc_ref, bn_ref, o_ref = refs
    else:
        q_ref, kc_ref, kn_ref, o_ref = refs
    rows = MLA_HEADS * tq
    q = q_ref[...].reshape(rows, QW)
    kc = kc_ref[0]
    kn = kn_ref[...]
    sc = _dot_nt(q, kc)
    sn = _dot_nt(q, kn)
    if has_bias:
        sc = sc + jnp.concatenate([bc_ref[...]] * MLA_HEADS, axis=0)
        sn = sn + jnp.concatenate([bn_ref[...]] * MLA_HEADS, axis=0)
    m = jnp.maximum(jnp.max(sc, axis=-1, keepdims=True), jnp.max(sn, axis=-1, keepdims=True))
    pc = jnp.exp2(sc - m)
    pn = jnp.exp2(sn - m)
    l = jnp.sum(pc, axis=-1, keepdims=True) + jnp.sum(pn, axis=-1, keepdims=True)
    o = (_dot(pc.astype(BF16), kc[:, :MLA_KV_LORA]) + _dot(pn.astype(BF16), kn[:, :MLA_KV_LORA])) / l
    for h in range(MLA_HEADS):
        o_ref[:, h * MLA_KV_LORA:(h + 1) * MLA_KV_LORA] = o[h * tq:(h + 1) * tq].astype(BF16)


def _mla_sample(q, kmat_new, kmat_cache, batch, seq, past_len):
    n = batch * seq
    q_pos = past_len + np.arange(seq)
    k_pos = np.arange(past_len + seq)
    mask = (k_pos[None, :] // CHUNK) <= (q_pos[:, None] // CHUNK)
    has_bias = not bool(mask.all())
    in_specs = [pl.BlockSpec((MLA_HEADS, seq, QW), lambda b: (0, b, 0)),
                pl.BlockSpec((1, past_len, QW), lambda b: (b, 0, 0)),
                pl.BlockSpec((seq, QW), lambda b: (b, 0))]
    args = [q, kmat_cache, kmat_new]
    if has_bias:
        bias = np.where(mask, 0.0, NEG_BIG).astype(np.float32)
        in_specs += [_const_spec((seq, past_len)), _const_spec((seq, seq))]
        args += [jnp.asarray(bias[:, :past_len]), jnp.asarray(bias[:, past_len:])]
    return pl.pallas_call(
        functools.partial(_mla_sample_body, tq=seq, has_bias=has_bias),
        grid=(batch,),
        in_specs=in_specs,
        out_specs=pl.BlockSpec((seq, MLA_HEADS * MLA_KV_LORA), lambda b: (b, 0)),
        out_shape=jax.ShapeDtypeStruct((n, MLA_HEADS * MLA_KV_LORA), BF16),
        compiler_params=_params(("parallel",)),
        name="mla_sample",
    )(*args)


def _mlstm_body(qk_ref, v_ref, og_ref, gates_ref, cw_ref, cb_ref, gb_ref, c0_ref, n0_ref, m0_ref, conv0_ref,
                out_ref, c1_ref, n1_ref, m1_ref, conv1_ref,
                xbuf, qk_s, g_s, cbd, n_s, m_s, *, tb, chunk):
    j = pl.program_id(1)
    H, DK, DV = ML_HEADS, ML_DK, ML_DV
    HD = H * DK
    L = chunk
    pad = 8

    @pl.when(j == 0)
    def _init():
        xbuf[0:pad, :] = conv0_ref[0]
        cbd[...] = jnp.zeros_like(cbd)
        for h in range(H):
            cbd[h * DK:(h + 1) * DK, h * DV:(h + 1) * DV] = c0_ref[0, h]
        n_s[...] = n0_ref[0]
        m_s[...] = m0_ref[0]

    _mlstm_conv(qk_ref, cw_ref, cb_ref, xbuf, qk_s, tb, pad)
    gl = gates_ref[...] + gb_ref[...]
    lane_g = lax.broadcasted_iota(jnp.int32, (1, LANES), 1)
    g_s[...] = jnp.where(lane_g < H, gl, jnp.minimum(gl, 0.0) - jnp.log1p(jnp.exp(-jnp.abs(gl))))

    ri = lax.broadcasted_iota(jnp.int32, (L, L), 0)
    ci = lax.broadcasted_iota(jnp.int32, (L, L), 1)
    tri = (ci <= ri).astype(BF16)
    sel = (lax.broadcasted_iota(jnp.int32, (8, LANES), 0) == lax.broadcasted_iota(jnp.int32, (8, LANES), 1)).astype(BF16)
    row_l = lax.broadcasted_iota(jnp.int32, (L, H * L), 0)
    col_l = lax.broadcasted_iota(jnp.int32, (L, H * L), 1)
    causal = (col_l % L) <= row_l
    grp_d = _lane_group(HD, DK)
    rgrp = lax.broadcasted_iota(jnp.int32, (HD, 1), 0) // DK
    bd_mask = rgrp == _lane_group(H * DV, DV)

    def chunk_step(c, carry):
        r0 = pl.multiple_of(c * L, L)
        q = qk_s[pl.ds(r0, L), 0:HD]
        k = qk_s[pl.ds(r0, L), HD:2 * HD]
        v = v_ref[pl.ds(r0, L), :]
        g = g_s[pl.ds(r0, L), :]
        cs = _sel_dot(tri, g)
        x = jnp.where(lane_g < H, g, cs)
        xt = _sel_dot_nt(sel, x)
        li_c = [x[:, h:h + 1] for h in range(H)]
        b_c = [x[:, H + h:H + h + 1] for h in range(H)]
        m_prev = [m_s[:, h:h + 1] for h in range(H)]
        row_e = jnp.concatenate([xt[h:h + 1, :] - xt[H + h:H + h + 1, :] for h in range(H)], axis=1)
        logw = jnp.where(causal, _expand_cols(b_c, L) + row_e, -jnp.inf)
        m_intra = _seg_max(logw, L, H)
        log_inter = [b_c[h] + m_prev[h] for h in range(H)]
        m_t = [jnp.maximum(log_inter[h], m_intra[h]) for h in range(H)]
        w = jnp.exp(logw - _expand_cols(m_t, L))
        qb = q.astype(BF16)
        kexp = jnp.concatenate([jnp.where(grp_d == h, k, 0.0) for h in range(H)], axis=0).astype(BF16)
        vexp = jnp.concatenate([jnp.where(grp_d == h, v, 0.0) for h in range(H)], axis=0).astype(BF16)
        p = _dot_nt(qb, kexp) * w
        den_intra = _seg_sum(p, L, H)
        num = _dot(p.astype(BF16), vexp)
        w_inter = [jnp.exp(log_inter[h] - m_t[h]) for h in range(H)]
        num = num + _expand_cols(w_inter, DV) * _dot(qb, cbd[...].astype(BF16))
        qn = _seg_sum(q * n_s[...], DK, H)
        inv = [1.0 / jnp.maximum(jnp.abs(den_intra[h] + w_inter[h] * qn[h]), jnp.exp(-m_t[h])) for h in range(H)]
        hout = num * _expand_cols(inv, DV)
        og = og_ref[pl.ds(r0, L), :]
        out_ref[pl.ds(r0, L), :] = (jax.nn.sigmoid(og) * hout).astype(BF16)
        m_new = [m_t[h][L - 1:L, :] for h in range(H)]
        b_last = [b_c[h][L - 1:L, :] for h in range(H)]
        w_s = [jnp.exp(b_last[h] - b_c[h] + li_c[h] - m_new[h]) for h in range(H)]
        decay = [jnp.exp(b_last[h] + m_prev[h] - m_new[h]) for h in range(H)]
        kw = k * _expand_cols(w_s, DK)
        upd = _dot_tn(kw.astype(BF16), v.astype(BF16))
        dcol = decay[H - 1]
        for h in range(H - 2, -1, -1):
            dcol = jnp.where(rgrp == h, decay[h], dcol)
        cbd[...] = dcol * cbd[...] + jnp.where(bd_mask, upd, 0.0)
        n_s[...] = _expand_cols(decay, DK) * n_s[...] + jnp.sum(kw, axis=0, keepdims=True)
        m_row = m_s[...]
        for h in range(H):
            m_row = jnp.where(lane_g == h, m_new[h], m_row)
        m_s[...] = m_row
        return carry

    lax.fori_loop(0, tb // L, chunk_step, 0)

    @pl.when(j == pl.num_programs(1) - 1)
    def _fin():
        for h in range(H):
            c1_ref[0, h] = cbd[h * DK:(h + 1) * DK, h * DV:(h + 1) * DV]
        n1_ref[0] = n_s[...]
        m1_ref[0] = m_s[...]
        conv1_ref[0] = xbuf[0:pad, :]


def _mlstm_conv(qk_ref, cw_ref, cb_ref, xbuf, qk_s, tb, pad):
    HD = ML_HEADS * ML_DK
    xbuf[pad:pad + tb, :] = qk_ref[...]
    conv = cb_ref[...] + cw_ref[ML_CONV - 1:ML_CONV, :] * xbuf[pad:pad + tb, :]
    for t in range(1, ML_CONV):
        conv = conv + cw_ref[ML_CONV - 1 - t:ML_CONV - t, :] * xbuf[pad - t:pad - t + tb, :]
    act = conv * jax.nn.sigmoid(conv)
    lane = lax.broadcasted_iota(jnp.int32, (1, 2 * HD), 1)
    qk_s[...] = jnp.where(lane < HD, act, act * (ML_DK ** -0.5))
    xbuf[0:pad, :] = xbuf[tb:tb + pad, :]


def _mlstm_fast_body(qk_ref, v_ref, og_ref, gates_ref, cw_ref, cb_ref, gbc_ref, utri_ref, ecols_ref,
                     c0_ref, n0_ref, mc0_ref, mr0_ref, conv0_ref,
                     out_ref, c1_ref, n1_ref, m1_ref, conv1_ref,
                     xbuf, qk_s, y_s, u_s, cb_s, ks_s, dec_s, nr_s, num_s, den_s, wi_s, cbd, n_s, m_c, mrow_s, *, tb):
    j = pl.program_id(1)
    H, DK, DV = ML_HEADS, ML_DK, ML_DV
    HD = H * DK
    L = DK
    nc = tb // L
    pad = 8

    @pl.when(j == 0)
    def _init():
        xbuf[0:pad, :] = conv0_ref[0]
        cbd[...] = jnp.zeros_like(cbd)
        for h in range(H):
            cbd[h * DK:(h + 1) * DK, h * DV:(h + 1) * DV] = c0_ref[0, h]
        n_s[...] = n0_ref[0]
        m_c[...] = mc0_ref[0]
        mrow_s[...] = mr0_ref[0]

    g8 = gates_ref[...].T[0:8, :] + gbc_ref[...]
    lsig = jnp.minimum(g8, 0.0) - jnp.log1p(jnp.exp(-jnp.abs(g8)))
    lf8 = pltpu.roll(lsig, 8 - H, axis=0)
    fh, fm, fl = _split3(lf8)
    b8 = _dot(fh, utri_ref[...]) + _dot(fm, utri_ref[...]) + _dot(fl, utri_ref[...])
    a8 = g8 - b8
    lane_t = lax.broadcasted_iota(jnp.int32, (1, tb), 1)
    cmax = a8
    sh = 1
    while sh < tb:
        cmax = jnp.maximum(cmax, jnp.where(lane_t >= sh, pltpu.roll(cmax, sh, axis=1), -jnp.inf))
        sh *= 2
    mp8 = jnp.maximum(m_c[:, 0:1], cmax)
    mt8 = b8 + mp8
    valid = lax.broadcasted_iota(jnp.int32, (8, 1), 0) < H
    x32 = jnp.concatenate([jnp.where(valid, x, 0.0) for x in (mp8, a8, jnp.exp(-mt8), mt8)], axis=0)
    xh, xm, xl = _split3(x32)
    ec = ecols_ref[...]
    y_s[...] = _dot_tn(xh, ec) + _dot_tn(xm, ec) + _dot_tn(xl, ec)
    m_c[...] = jnp.broadcast_to(mt8[:, tb - 1:tb], m_c.shape)

    _mlstm_conv(qk_ref, cw_ref, cb_ref, xbuf, qk_s, tb, pad)

    grp = _lane_group(HD, DK)
    rgrp = lax.broadcasted_iota(jnp.int32, (HD, 1), 0) // DK
    bd_mask = rgrp == grp
    seg_b = bd_mask.astype(BF16)
    row_l = lax.broadcasted_iota(jnp.int32, (L, HD), 0)
    col_l = lax.broadcasted_iota(jnp.int32, (L, HD), 1) % L
    causal = col_l <= row_l
    diag = col_l == row_l

    for c in range(nc):
        r0 = c * L
        q = qk_s[r0:r0 + L, 0:HD]
        k = qk_s[r0:r0 + L, HD:2 * HD]
        vb = v_ref[r0:r0 + L, :].astype(BF16)
        m_e = y_s[r0:r0 + L, 0:HD]
        a_e = y_s[r0:r0 + L, HD:2 * HD]
        m_prev = mrow_s[...] if c == 0 else y_s[r0 - 1:r0, 0:HD]
        m_last = m_e[L - 1:L, :]
        row_a = jnp.sum(jnp.where(diag, a_e, 0.0), axis=0, keepdims=True)
        d = jnp.exp(jnp.where(causal, row_a - m_e, -jnp.inf))
        qb = q.astype(BF16)
        kb = k.astype(BF16)
        kexp = jnp.concatenate([jnp.where(grp == h, kb, jnp.zeros_like(kb)) for h in range(H)], axis=0)
        vexp = jnp.concatenate([jnp.where(grp == h, vb, jnp.zeros_like(vb)) for h in range(H)], axis=0)
        p = _dot_nt(qb, kexp) * d
        p_hi = p.astype(BF16)
        p_lo = (p - p_hi.astype(F32)).astype(BF16)
        num_s[r0:r0 + L, :] = _dot(p_hi, vexp)
        den_s[r0:r0 + L, :] = _dot(p_hi, seg_b) + _dot(p_lo, seg_b)
        wi_s[r0:r0 + L, :] = jnp.exp(m_prev - m_e)
        kw = k * jnp.exp(a_e - m_last)
        u_s[c] = jnp.where(bd_mask, _dot_tn(kw.astype(BF16), vb), 0.0)
        ks_s[c:c + 1, :] = jnp.sum(kw, axis=0, keepdims=True)
        dec_s[c:c + 1, :] = jnp.exp(m_prev - m_last)

    for c in range(nc):
        cb_s[c] = cbd[...].astype(BF16)
        nr_s[c:c + 1, :] = n_s[...]
        dec = dec_s[c:c + 1, :]
        cbd[...] = dec * cbd[...] + u_s[c]
        n_s[...] = dec * n_s[...] + ks_s[c:c + 1, :]

    for c in range(nc):
        r0 = c * L
        q = qk_s[r0:r0 + L, 0:HD]
        qn = q * nr_s[c:c + 1, :]
        qn_hi = qn.astype(BF16)
        qn_lo = (qn - qn_hi.astype(F32)).astype(BF16)
        w_inter = wi_s[r0:r0 + L, :]
        num = num_s[r0:r0 + L, :] + w_inter * _dot(q.astype(BF16), cb_s[c])
        den = den_s[r0:r0 + L, :] + w_inter * (_dot(qn_hi, seg_b) + _dot(qn_lo, seg_b))
        hout = num / jnp.maximum(jnp.abs(den), y_s[r0:r0 + L, 2 * HD:3 * HD])
        og = og_ref[r0:r0 + L, :]
        out_ref[r0:r0 + L, :] = (jax.nn.sigmoid(og) * hout).astype(BF16)

    mrow_s[...] = y_s[tb - 1:tb, 3 * HD:4 * HD]

    @pl.when(j == pl.num_programs(1) - 1)
    def _fin():
        for h in range(H):
            c1_ref[0, h] = cbd[h * DK:(h + 1) * DK, h * DV:(h + 1) * DV]
        n1_ref[0] = n_s[...]
        m1_ref[0] = m_c[...]
        conv1_ref[0] = xbuf[0:pad, :]


def _mlstm_fast(qk, v, og, gates, conv_w, conv_b, gate_bias, c0, n0, m0, conv0, batch, seq, tb):
    n = batch * seq
    H, DK, DV = ML_HEADS, ML_DK, ML_DV
    HD = H * DK
    nb = seq // tb
    nc = tb // DK
    pad = 8
    n0p = n0.reshape(batch, 1, HD)
    mc0 = jnp.broadcast_to(jnp.pad(m0, ((0, 0), (0, 8 - H)))[:, :, None], (batch, 8, LANES))
    mr0 = jnp.repeat(m0, DV, axis=1).reshape(batch, 1, HD)
    conv0p = jnp.pad(conv0, ((0, 0), (pad - (ML_CONV - 1), 0), (0, 0)))
    gbc = jnp.concatenate([gate_bias[0], gate_bias[1]])[:, None]
    idx = np.arange(tb)
    utri = jnp.asarray(idx[:, None] <= idx[None, :], BF16)
    ecols_np = np.zeros((32, 4 * HD), np.float32)
    for kq in range(4):
        for h in range(H):
            ecols_np[8 * kq + h, kq * HD + h * DV:kq * HD + (h + 1) * DV] = 1.0
    ecols = jnp.asarray(ecols_np, BF16)
    row = lambda w: pl.BlockSpec((tb, w), lambda b, j: (b * nb + j, 0))
    per_b = lambda shape: pl.BlockSpec((1,) + shape, lambda b, j: (b,) + (0,) * len(shape))
    f32 = lambda *shape: pltpu.VMEM(shape, F32)
    out, c1, n1, m1, conv1 = pl.pallas_call(
        functools.partial(_mlstm_fast_body, tb=tb),
        grid=(batch, nb),
        in_specs=[row(2 * HD), row(H * DV), row(H * DV), row(LANES),
                  _const_spec((ML_CONV, 2 * HD)), _const_spec((1, 2 * HD)), _const_spec((2 * H, 1)),
                  _const_spec((tb, tb)), _const_spec((32, 4 * HD)),
                  per_b((H, DK, DV)), per_b((1, HD)), per_b((8, LANES)), per_b((1, HD)), per_b((pad, 2 * HD))],
        out_specs=[row(H * DV), per_b((H, DK, DV)), per_b((1, HD)), per_b((8, LANES)), per_b((pad, 2 * HD))],
        out_shape=[jax.ShapeDtypeStruct((n, H * DV), BF16), jax.ShapeDtypeStruct((batch, H, DK, DV), F32),
                   jax.ShapeDtypeStruct((batch, 1, HD), F32), jax.ShapeDtypeStruct((batch, 8, LANES), F32),
                   jax.ShapeDtypeStruct((batch, pad, 2 * HD), F32)],
        scratch_shapes=[f32(tb + pad, 2 * HD), f32(tb, 2 * HD), f32(tb, 4 * HD), f32(nc, HD, H * DV),
                        pltpu.VMEM((nc, HD, H * DV), BF16), f32(nc, HD), f32(nc, HD), f32(nc, HD),
                        f32(tb, HD), f32(tb, HD), f32(tb, HD), f32(HD, H * DV), f32(1, HD), f32(8, LANES),
                        f32(1, HD)],
        compiler_params=_params(("parallel", "arbitrary")),
        name="mlstm_blocked",
    )(qk, v, og, gates, conv_w, conv_b, gbc, utri, ecols, c0, n0p, mc0, mr0, conv0p)
    return (out, c1, n1.reshape(batch, H, DK), m1[:, :H, 0], conv1[:, pad - (ML_CONV - 1):, :])


def _mlstm(qk, v, og, gates, conv_w, conv_b, gate_bias, c0, n0, m0, conv0, batch, seq):
    if ML_DK == ML_DV == CHUNK and seq % 512 == 0:
        return _mlstm_fast(qk, v, og, gates, conv_w, conv_b, gate_bias, c0, n0, m0, conv0, batch, seq, 512)
    gb = gate_bias
    gbias = jnp.concatenate([gb[0], gb[1], jnp.zeros((LANES - 2 * ML_HEADS,), F32)])[None, :]
    n = batch * seq
    H, DK, DV = ML_HEADS, ML_DK, ML_DV
    chunk = min(CHUNK, seq)
    assert seq % chunk == 0 and seq >= ML_CONV - 1 and chunk % 8 == 0
    tb = 512 if seq % 512 == 0 else chunk
    nb = seq // tb
    pad = 8
    n0p = n0.reshape(batch, 1, H * DK)
    m0p = jnp.pad(m0.reshape(batch, 1, H), ((0, 0), (0, 0), (0, LANES - H)))
    conv0p = jnp.pad(conv0, ((0, 0), (pad - (ML_CONV - 1), 0), (0, 0)))
    row = lambda w: pl.BlockSpec((tb, w), lambda b, j: (b * nb + j, 0))
    per_b = lambda shape: pl.BlockSpec((1,) + shape, lambda b, j: (b,) + (0,) * len(shape))
    out, c1, n1, m1, conv1 = pl.pallas_call(
        functools.partial(_mlstm_body, tb=tb, chunk=chunk),
        grid=(batch, nb),
        in_specs=[row(2 * H * DK), row(H * DV), row(H * DV), row(LANES),
                  _const_spec((ML_CONV, 2 * H * DK)), _const_spec((1, 2 * H * DK)), _const_spec((1, LANES)),
                  per_b((H, DK, DV)), per_b((1, H * DK)), per_b((1, LANES)), per_b((pad, 2 * H * DK))],
        out_specs=[row(H * DV), per_b((H, DK, DV)), per_b((1, H * DK)), per_b((1, LANES)), per_b((pad, 2 * H * DK))],
        out_shape=[jax.ShapeDtypeStruct((n, H * DV), BF16), jax.ShapeDtypeStruct((batch, H, DK, DV), F32),
                   jax.ShapeDtypeStruct((batch, 1, H * DK), F32), jax.ShapeDtypeStruct((batch, 1, LANES), F32),
                   jax.ShapeDtypeStruct((batch, pad, 2 * H * DK), F32)],
        scratch_shapes=[pltpu.VMEM((tb + pad, 2 * H * DK), F32), pltpu.VMEM((tb, 2 * H * DK), F32),
                        pltpu.VMEM((tb, LANES), F32), pltpu.VMEM((H * DK, H * DV), F32),
                        pltpu.VMEM((1, H * DK), F32), pltpu.VMEM((1, LANES), F32)],
        compiler_params=_params(("parallel", "arbitrary")),
        name="mlstm",
    )(qk, v, og, gates, conv_w, conv_b, gbias, c0, n0p, m0p, conv0p)
    return (out, c1, n1.reshape(batch, H, DK), m1[:, 0, :H], conv1[:, pad - (ML_CONV - 1):, :])


def _hgrn_body(hg_ref, lbraw_ref, hnorm_ref, tri_ref, s0_ref, out_ref, s1_ref,
               hq_s, kk_s, f_s, b_s, qe_s, oi_s, u_s, sts_s, dec_s, st, *, tb, sub, layer):
    j = pl.program_id(1)
    H, DK, DV = HG_HEADS, HG_DK, HG_DV
    HD = H * DK
    S = sub

    @pl.when(j == 0)
    def _init():
        st[...] = jnp.zeros_like(st)
        for h in range(H):
            st[h * DV:(h + 1) * DV, h * DK:(h + 1) * DK] = s0_ref[0, h].T

    raw = lbraw_ref[...]
    e = jnp.exp(raw - jnp.max(raw, axis=0, keepdims=True))
    sm = e / jnp.sum(e, axis=0, keepdims=True)
    cum = sm[0:1, :]
    for l in range(1, layer + 1):
        cum = cum + sm[l:l + 1, :]
    lb = cum - sm[0:1, :]

    z = hg_ref[:, HD:2 * HD]
    lsig = jnp.minimum(z, 0.0) - jnp.log1p(jnp.exp(-jnp.abs(z)))
    a1 = jnp.log(lb)
    a2 = jnp.log1p(-lb) + lsig
    delta = a1 - a2
    lf = jnp.where(jnp.isnan(delta), a1 + a2, jnp.maximum(a1, a2) + jnp.log1p(jnp.exp(-jnp.abs(delta))))
    f_s[...] = jnp.exp(lf)
    b_all = _sel_dot(tri_ref[...], lf)
    b_s[...] = b_all
    kk_s[...] = (1.0 - lb) * jax.nn.sigmoid(-z)
    xq = hg_ref[:, 0:HD]
    hq_all = xq * jax.nn.sigmoid(xq)
    hq_s[...] = hq_all
    qe_s[...] = (hq_all * jnp.exp(b_all)).astype(BF16)

    rowi = lax.broadcasted_iota(jnp.int32, (S, 1), 0)
    seg = ((lax.broadcasted_iota(jnp.int32, (HD, 1), 0) // DK) == _lane_group(H * DV, DV))
    seg_b = seg.astype(BF16)
    gnorm = hnorm_ref[...]

    n_sub = tb // S

    for c in range(n_sub):
        r0 = c * S
        hq = hq_s[r0:r0 + S, :]
        kk = kk_s[r0:r0 + S, :]
        f = f_s[r0:r0 + S, :]
        b = b_s[r0:r0 + S, :]
        iv = hg_ref[r0:r0 + S, 2 * HD:2 * HD + H * DV]
        bl = b[S - 1:S, :]
        kd = kk * jnp.exp(bl - b)
        u_s[c] = jnp.where(seg, _dot_tn(iv.astype(BF16), kd.astype(BF16)), 0.0)
        dec_s[c:c + 1, :] = jnp.exp(bl)
        qd = jnp.where(rowi == S - 1, hq, 0.0)
        ws = [None] * S
        ws[S - 1] = (qd * kk[S - 1:S, :]).astype(BF16)
        for s in range(S - 2, -1, -1):
            qd = jnp.where(rowi == s, hq, qd * f[s + 1:s + 2, :])
            ws[s] = (qd * kk[s:s + 1, :]).astype(BF16)
        r = _dot(jnp.concatenate(ws, axis=0), seg_b)
        oi = r[0:S, :] * iv[0:1, :]
        for s in range(1, S):
            oi = oi + r[s * S:(s + 1) * S, :] * iv[s:s + 1, :]
        oi_s[r0:r0 + S, :] = oi

    for c in range(n_sub):
        sts_s[c] = st[...].astype(BF16)
        st[...] = dec_s[c:c + 1, :] * st[...] + u_s[c]

    grp_rows = min(n_sub, 4) * S
    for c0 in range(0, n_sub, grp_rows // S):
        r0 = c0 * S
        o = jnp.concatenate([oi_s[(c0 + i) * S:(c0 + i + 1) * S, :]
                             + _dot_nt(qe_s[(c0 + i) * S:(c0 + i + 1) * S, :], sts_s[c0 + i])
                             for i in range(grp_rows // S)], axis=0)
        gt = hg_ref[r0:r0 + grp_rows, 2 * HD + H * DV:2 * HD + 2 * H * DV]
        ms = _dot((o * o).astype(BF16), seg_b) * (1.0 / DV)
        y = o * lax.rsqrt(ms + EPS) * gnorm * (gt * jax.nn.sigmoid(gt))
        out_ref[r0:r0 + grp_rows, :] = y.astype(BF16)

    @pl.when(j == pl.num_programs(1) - 1)
    def _fin():
        for h in range(H):
            s1_ref[0, h] = st[h * DV:(h + 1) * DV, h * DK:(h + 1) * DK].T


def _hgrn(hg, lb_raw, hnorm, s0, batch, seq, layer):
    n = batch * seq
    H, DK, DV = HG_HEADS, HG_DK, HG_DV
    assert DK == DV
    sub = min(HG_SUB, seq)
    assert seq % sub == 0 and sub % 8 == 0
    tb = 512 if seq % 512 == 0 else sub
    nb = seq // tb
    depth = lb_raw.shape[0]
    n_sub = tb // sub
    idx = np.arange(tb)
    tri = jnp.asarray((idx[:, None] // sub == idx[None, :] // sub) & (idx[None, :] <= idx[:, None]), BF16)
    per_b = pl.BlockSpec((1, H, DK, DV), lambda b, j: (b, 0, 0, 0))
    out, s1 = pl.pallas_call(
        functools.partial(_hgrn_body, tb=tb, sub=sub, layer=layer),
        grid=(batch, nb),
        in_specs=[pl.BlockSpec((tb, 4 * H * DK), lambda b, j: (b * nb + j, 0)), _const_spec((depth, H * DK)),
                  _const_spec((1, H * DV)), _const_spec((tb, tb)), per_b],
        out_specs=[pl.BlockSpec((tb, H * DV), lambda b, j: (b * nb + j, 0)), per_b],
        out_shape=[jax.ShapeDtypeStruct((n, H * DV), BF16), jax.ShapeDtypeStruct((batch, H, DK, DV), F32)],
        scratch_shapes=[pltpu.VMEM((tb, H * DK), F32), pltpu.VMEM((tb, H * DK), F32), pltpu.VMEM((tb, H * DK), F32),
                        pltpu.VMEM((tb, H * DK), F32), pltpu.VMEM((tb, H * DK), BF16),
                        pltpu.VMEM((tb, H * DV), F32), pltpu.VMEM((n_sub, H * DV, H * DK), F32),
                        pltpu.VMEM((n_sub, H * DV, H * DK), BF16), pltpu.VMEM((max(n_sub, 8), H * DK), F32),
                        pltpu.VMEM((H * DV, H * DK), F32)],
        compiler_params=_params(("parallel", "arbitrary")),
        name="hgrn2",
    )(hg, lb_raw, hnorm, tri, s0)
    return out, s1


def _outproj_ffn_body(x_ref, o_ref, ml_ref, hgo_ref, g_ref, w1_ref, w2_ref, w3_ref,
                      gpre_ref, gpost_ref, wg_ref, wu_ref, wo_ref, y_ref, xn_ref, acc_ref, *, fc):
    mix = _dot(o_ref[...], w1_ref[...]) + _dot(ml_ref[...], w2_ref[...]) + _dot(hgo_ref[...], w3_ref[...])
    x1 = x_ref[...] + _rms(mix, g_ref[...])
    y_ref[...] = _ffn_apply(x1, gpre_ref, gpost_ref, wg_ref, wu_ref, wo_ref, xn_ref, acc_ref, fc)


def _outproj_ffn(x, o_lat, ml_out, hg_out, g, w1, w2, w3, g_pre, g_post, w_gate, w_up, w_out):
    n, d = x.shape
    d_ff = w_gate.shape[1]
    tm = _token_tile(n)
    fc = 256 if d_ff % 256 == 0 else d_ff
    row = lambda w: pl.BlockSpec((tm, w), lambda i: (i, 0))
    return pl.pallas_call(
        functools.partial(_outproj_ffn_body, fc=fc),
        grid=(n // tm,),
        in_specs=[row(d), row(o_lat.shape[1]), row(ml_out.shape[1]), row(hg_out.shape[1]), _const_spec((1, d)),
                  _const_spec(w1.shape), _const_spec(w2.shape), _const_spec(w3.shape),
                  _const_spec((1, d)), _const_spec((1, d)), _const_spec((d, d_ff)), _const_spec((d, d_ff)),
                  _const_spec((d_ff, d))],
        out_specs=row(d),
        out_shape=jax.ShapeDtypeStruct((n, d), F32),
        scratch_shapes=[pltpu.VMEM((tm, d), BF16), pltpu.VMEM((tm, d), F32)],
        compiler_params=_params(("parallel",)),
        name="out_proj_ffn",
    )(x, o_lat, ml_out, hg_out, g, w1, w2, w3, g_pre, g_post, w_gate, w_up, w_out)


def _rope_tables(pos):
    inv = ROPE_THETA ** (-jnp.arange(0, MLA_ROPE, 2, dtype=F32) / MLA_ROPE)
    ang = pos.astype(F32)[:, None] * inv[None, :]
    cos, sin = jnp.cos(ang), jnp.sin(ang)
    t = pos.shape[0]
    cos2, sin2 = jnp.concatenate([cos, cos], -1), jnp.concatenate([sin, sin], -1)
    tabq = (MLA_SCALE * LOG2E) * jnp.concatenate([jnp.ones((t, MLA_KV_LORA), F32), cos2, sin2,
                                        jnp.zeros((t, QW - MLA_KV_LORA - 2 * MLA_ROPE), F32)], -1)
    zpad = jnp.zeros((t, LANES - 2 * MLA_ROPE), F32)
    cos_t = jnp.concatenate([cos2, cos2, zpad], -1)
    sin_t = jnp.concatenate([sin2, sin2, zpad], -1)
    return tabq, cos_t, sin_t


def _swap_halves(w):
    half = w.shape[-1] // 2
    return jnp.concatenate([-w[..., half:], w[..., :half]], axis=-1)


def _layer_weights(l, ln_gains, w_ffn_in, w_ffn_out, w_in, w_out, mla_q_norm, mla_kv_norm, mla_w_uq, mla_w_uk,
                   mla_w_uv, ml_conv_w, ml_conv_b, ml_gate_bias, hg_norm):
    d = w_in.shape[1]
    d_ff = w_ffn_out.shape[2]
    sizes = (MLA_Q_LORA, MLA_KV_LORA, MLA_ROPE, 2 * ML_HEADS * ML_DK, ML_HEADS * ML_DV, ML_HEADS * ML_DV, ML_HEADS,
             ML_HEADS, HG_HEADS * HG_DK, HG_HEADS * HG_DK, HG_HEADS * HG_DV, HG_HEADS * HG_DV)
    pts = np.cumsum((0,) + sizes)
    part = [w_in[l][:, pts[i]:pts[i + 1]] for i in range(len(sizes))]
    zeros = lambda n: jnp.zeros((d, n), F32)
    kpe, kpe_sw = part[2], _swap_halves(part[2])
    w_perm = jnp.concatenate(
        [part[0], part[1], kpe, kpe, zeros(LANES - 2 * MLA_ROPE), kpe_sw, kpe_sw, zeros(LANES - 2 * MLA_ROPE),
         part[3], part[4], part[5], part[6], part[7], zeros(LANES - 2 * ML_HEADS), part[8], part[9], part[10],
         part[11]], axis=1).astype(BF16)
    assert w_perm.shape[1] == _IN_COLS
    uq = mla_w_uq[l].reshape(MLA_Q_LORA, MLA_HEADS, MLA_NOPE + MLA_ROPE)
    uq_nope = jnp.transpose(uq[:, :, :MLA_NOPE], (1, 0, 2))
    uk_t = jnp.transpose(mla_w_uk[l], (1, 2, 0))
    w_lat = _bmm(uq_nope, uk_t)
    uq_pe = jnp.transpose(uq[:, :, MLA_NOPE:], (1, 0, 2))
    wq = jnp.concatenate([w_lat, uq_pe, _swap_halves(uq_pe),
                          jnp.zeros((MLA_HEADS, MLA_Q_LORA, QW - MLA_KV_LORA - 2 * MLA_ROPE), F32)], axis=-1)
    wq = jnp.transpose(wq, (1, 0, 2)).reshape(MLA_Q_LORA, MLA_HEADS * QW).astype(BF16)
    n_mla = MLA_HEADS * MLA_V
    uv = jnp.transpose(mla_w_uv[l], (1, 0, 2))
    wo_mla = w_out[l][:n_mla].reshape(MLA_HEADS, MLA_V, d)
    w1 = _bmm(uv, wo_mla).reshape(MLA_HEADS * MLA_KV_LORA, d).astype(BF16)
    n_ml = ML_HEADS * ML_DV
    w2 = w_out[l][n_mla:n_mla + n_ml].astype(BF16)
    w3 = w_out[l][n_mla + n_ml:].astype(BF16)
    ffn = []
    for j in range(2):
        wi = w_ffn_in[l, j]
        ffn.append((wi[:, :d_ff].astype(BF16), wi[:, d_ff:].astype(BF16), w_ffn_out[l, j].astype(BF16)))
    return dict(ln=ln_gains[l], ffn=ffn, w_perm=w_perm, wq=wq, w1=w1, w2=w2, w3=w3,
                q_norm=mla_q_norm[l][None, :], kv_norm=mla_kv_norm[l][None, :], conv_w=ml_conv_w[l],
                conv_b=ml_conv_b[l][None, :], gate_bias=ml_gate_bias[l].astype(F32), hnorm=jnp.tile(hg_norm[l], HG_HEADS)[None, :])


def _tile_rows(t, reps):
    return jnp.tile(t, (reps, 1)) if reps > 1 else t


def _layer(x, lw, layer, depth, lb_raw, batch, seq, pos, past, stacks):
    n, d = x.shape
    g = lw['ln']
    gain = lambda i: g[i][None, :]
    x = _ffn(x, gain(0), gain(1), *lw['ffn'][0])
    tabq, cos_t, sin_t = _rope_tables(pos)
    tm = _token_tile(n)
    reps = tm // seq if seq < tm else 1
    tabs = [_tile_rows(t, reps) for t in (tabq, cos_t, sin_t)]
    q, kmat, ckv, kpe, mlqk, mlv, mlo, gates, hg = _inproj(x, gain(2), lw['w_perm'], lw['q_norm'], lw['kv_norm'],
                                                          lw['wq'], *tabs, seq, layer, depth, stacks)
    if past is None:
        o_lat = _mla_prompt(q, kmat, batch, seq)
        c0 = jnp.zeros((batch, ML_HEADS, ML_DK, ML_DV), F32)
        n0 = jnp.zeros((batch, ML_HEADS, ML_DK), F32)
        m0 = jnp.zeros((batch, ML_HEADS), F32)
        conv0 = jnp.zeros((batch, ML_CONV - 1, 2 * ML_HEADS * ML_DK), F32)
        s0 = jnp.zeros((batch, HG_HEADS, HG_DK, HG_DV), F32)
    else:
        cache_ckv, cache_kpe, c0, n0, m0, conv0, s0 = past
        past_len = cache_ckv.shape[1]
        kmat_cache = jnp.concatenate(
            [cache_ckv, cache_kpe, cache_kpe,
             jnp.ones((batch, past_len, 1), F32),
             jnp.zeros((batch, past_len, QW - _K_ONE - 1), F32)], axis=-1).astype(BF16)
        o_lat = _mla_sample(q, kmat, kmat_cache, batch, seq, past_len)
    ml_out, c1, n1, m1, conv1 = _mlstm(mlqk, mlv, mlo, gates, lw['conv_w'], lw['conv_b'], lw['gate_bias'],
                                       c0, n0, m0, conv0, batch, seq)
    hg_out, s1 = _hgrn(hg, lb_raw, lw['hnorm'], s0, batch, seq, layer)
    x = _outproj_ffn(x, o_lat, ml_out, hg_out, gain(3), lw['w1'], lw['w2'], lw['w3'],
                     gain(4), gain(5), *lw['ffn'][1])
    return x, (ckv, kpe), (c1, n1, m1, conv1, s1)


def kernel(x_prompt, x_sample, cache_ckv, cache_kpe, state_mlstm_c, state_mlstm_n, state_mlstm_m, state_mlstm_conv, state_hgrn, ln_gains, w_ffn_in, w_ffn_out, w_in, w_out, mla_q_norm, mla_kv_norm, mla_w_uq, mla_w_uk, mla_w_uv, ml_conv_w, ml_conv_b, ml_gate_bias, hg_lb_raw, hg_norm):
    depth = w_in.shape[0]
    bp, tp, d = x_prompt.shape
    bs, ts, _ = x_sample.shape
    past_len = cache_ckv.shape[2]
    pos_p = jnp.arange(tp)
    pos_s = past_len + jnp.arange(ts)
    yp = x_prompt.reshape(bp * tp, d)
    ys = x_sample.reshape(bs * ts, d)
    lb_raw = hg_lb_raw.astype(F32)
    p_states, s_states = [], []
    p_kv = s_kv = None
    for l in range(depth):
        lw = _layer_weights(l, ln_gains, w_ffn_in, w_ffn_out, w_in, w_out, mla_q_norm, mla_kv_norm, mla_w_uq,
                            mla_w_uk, mla_w_uv, ml_conv_w, ml_conv_b, ml_gate_bias, hg_norm)
        yp, p_kv, st_p = _layer(yp, lw, l, depth, lb_raw, bp, tp, pos_p, None, p_kv)
        past = (cache_ckv[l], cache_kpe[l], state_mlstm_c[l], state_mlstm_n[l], state_mlstm_m[l],
                state_mlstm_conv[l], state_hgrn[l])
        ys, s_kv, st_s = _layer(ys, lw, l, depth, lb_raw, bs, ts, pos_s, past, s_kv)
        p_states.append(st_p)
        s_states.append(st_s)
    p_out = [jnp.stack([st[i] for st in p_states]) for i in range(5)]
    s_out = [jnp.stack([st[i] for st in s_states]) for i in range(5)]
    return (yp.reshape(bp, tp, d), ys.reshape(bs, ts, d),
            p_kv[0].reshape(depth, bp, tp, MLA_KV_LORA), p_kv[1].reshape(depth, bp, tp, MLA_ROPE), *p_out,
            s_kv[0].reshape(depth, bs, ts, MLA_KV_LORA), s_kv[1].reshape(depth, bs, ts, MLA_ROPE), *s_out)
```

```python
import functools

import numpy as np
import jax
import jax.numpy as jnp
from jax import lax
from jax.experimental import pallas as pl
from jax.experimental.pallas import tpu as pltpu

F32 = jnp.float32
BF16 = jnp.bfloat16

CHUNK = 64
FFN_RES = 0.5
EPS = 1e-6
MLA_HEADS = 8
MLA_NOPE = 64
MLA_ROPE = 32
MLA_V = 64
MLA_Q_LORA = 256
MLA_KV_LORA = 128
ROPE_THETA = 10000.0
MLA_SCALE = (MLA_NOPE + MLA_ROPE) ** -0.5
ML_HEADS = 4
ML_DK = 64
ML_DV = 64
ML_CONV = 4
HG_HEADS = 4
HG_DK = 64
HG_DV = 64

LANES = 128
VMEM_LIMIT = 56 * 1024 * 1024
QW = 2 * LANES
_K_ONE = MLA_KV_LORA + 2 * MLA_ROPE
LOG2E = 1.4426950408889634
HG_SUB = 16
NEG_BIG = -1e30


def _rms(x, g):
    return x * lax.rsqrt(jnp.mean(x * x, axis=-1, keepdims=True) + EPS) * g


def _dot(a, b):
    return jnp.dot(a, b, preferred_element_type=F32)


def _dot_nt(a, b):
    return lax.dot_general(a, b, (((1,), (1,)), ((), ())), preferred_element_type=F32)


def _dot_tn(a, b):
    return lax.dot_general(a, b, (((0,), (0,)), ((), ())), preferred_element_type=F32)


def _split3(x):
    hi = x.astype(BF16)
    r = x - hi.astype(F32)
    mid = r.astype(BF16)
    lo = (r - mid.astype(F32)).astype(BF16)
    return hi, mid, lo


def _sel_dot(sel, x):
    hi, mid, lo = _split3(x)
    return _dot(sel, hi) + _dot(sel, mid) + _dot(sel, lo)


def _sel_dot_nt(sel, x):
    hi, mid, lo = _split3(x)
    return _dot_nt(sel, hi) + _dot_nt(sel, mid) + _dot_nt(sel, lo)


def _lane_group(n, g):
    return lax.broadcasted_iota(jnp.int32, (1, n), 1) // g


def _expand_cols(cols, g):
    grp = _lane_group(len(cols) * g, g)
    out = cols[-1]
    for h in range(len(cols) - 2, -1, -1):
        out = jnp.where(grp == h, cols[h], out)
    return out


def _seg_sum(x, g, heads):
    grp = _lane_group(heads * g, g)
    return [jnp.sum(jnp.where(grp == h, x, 0.0), axis=-1, keepdims=True) for h in range(heads)]


def _seg_max(x, g, heads):
    grp = _lane_group(heads * g, g)
    return [jnp.max(jnp.where(grp == h, x, -jnp.inf), axis=-1, keepdims=True) for h in range(heads)]


def _const_spec(shape):
    nd = len(shape)
    return pl.BlockSpec(shape, lambda *_: (0,) * nd, pipeline_mode=pl.Buffered(1))


def _params(sem):
    return pltpu.CompilerParams(dimension_semantics=sem, vmem_limit_bytes=VMEM_LIMIT)


def _token_tile(n):
    for t in (512, 256, 128, 64, 32, 16, 8):
        if n % t == 0:
            return t
    raise ValueError(f"token count {n} must be a multiple of 8")


def _bmm_body(a_ref, b_ref, o_ref):
    o_ref[0] = jnp.dot(a_ref[0], b_ref[0], preferred_element_type=F32, precision=lax.Precision.HIGHEST)


def _bmm(a, b):
    h, m, k = a.shape
    n = b.shape[2]
    return pl.pallas_call(
        _bmm_body,
        grid=(h,),
        in_specs=[pl.BlockSpec((1, m, k), lambda i: (i, 0, 0)), pl.BlockSpec((1, k, n), lambda i: (i, 0, 0))],
        out_specs=pl.BlockSpec((1, m, n), lambda i: (i, 0, 0)),
        out_shape=jax.ShapeDtypeStruct((h, m, n), F32),
        compiler_params=_params(("parallel",)),
        name="weight_fold",
    )(a, b)


def _ffn_apply(x, gpre_ref, gpost_ref, wg_ref, wu_ref, wo_ref, xn_ref, acc_ref, fc):
    xn_ref[...] = _rms(x, gpre_ref[...]).astype(BF16)
    d_ff = wg_ref.shape[1]
    for c in range(d_ff // fc):
        xn = xn_ref[...]
        sl = slice(c * fc, (c + 1) * fc)
        gate = _dot(xn, wg_ref[:, sl])
        up = _dot(xn, wu_ref[:, sl])
        act = (gate * jax.nn.sigmoid(gate) * up).astype(BF16)
        part = _dot(act, wo_ref[sl, :])
        if c == 0:
            acc_ref[...] = part
        else:
            acc_ref[...] += part
    return x + FFN_RES * _rms(acc_ref[...], gpost_ref[...])


def _ffn_body(x_ref, gpre_ref, gpost_ref, wg_ref, wu_ref, wo_ref, o_ref, xn_ref, acc_ref, *, fc):
    o_ref[...] = _ffn_apply(x_ref[...], gpre_ref, gpost_ref, wg_ref, wu_ref, wo_ref, xn_ref, acc_ref, fc)


def _ffn(x, g_pre, g_post, w_gate, w_up, w_out):
    n, d = x.shape
    d_ff = w_gate.shape[1]
    tm = _token_tile(n)
    fc = 256 if d_ff % 256 == 0 else d_ff
    row = pl.BlockSpec((tm, d), lambda i: (i, 0))
    return pl.pallas_call(
        functools.partial(_ffn_body, fc=fc),
        grid=(n // tm,),
        in_specs=[row, _const_spec((1, d)), _const_spec((1, d)), _const_spec((d, d_ff)), _const_spec((d, d_ff)),
                  _const_spec((d_ff, d))],
        out_specs=row,
        out_shape=jax.ShapeDtypeStruct((n, d), F32),
        scratch_shapes=[pltpu.VMEM((tm, d), BF16), pltpu.VMEM((tm, d), F32)],
        compiler_params=_params(("parallel",)),
        name="ffn",
    )(x, g_pre, g_post, w_gate, w_up, w_out)


_C_CQ = (0, 256)
_C_CKV = (256, 384)
_C_ROPE_A = (384, 512)
_C_ROPE_B = (512, 640)
_C_MLQK = (640, 1152)
_C_MLV = (1152, 1408)
_C_MLO = (1408, 1664)
_C_GATES = (1664, 1792)
_C_HG = (1792, 2816)
_IN_COLS = 2816


def _inproj_body(x_ref, g_ref, w_ref, qn_ref, kvn_ref, wq_ref, tabq_ref, cos_ref, sin_ref, *rest):
    q_ref, kmat_ref, ckv_ref, kpe_ref, mlqk_ref, mlv_ref, mlo_ref, gates_ref, hgf_ref, hgr_ref, hn_ref = rest[-11:]
    hn_ref[...] = _rms(x_ref[...], g_ref[...]).astype(BF16)

    def proj(cols):
        return _dot(hn_ref[...], w_ref[:, cols[0]:cols[1]])

    cq = proj(_C_CQ)
    ckv = _rms(proj(_C_CKV), kvn_ref[...])
    ckv_ref[0] = ckv
    rot = proj(_C_ROPE_A) * cos_ref[...] + proj(_C_ROPE_B) * sin_ref[...]
    kpe_ref[0] = rot[:, :MLA_ROPE]
    kmat_ref[:, :MLA_KV_LORA] = ckv.astype(BF16)
    lane = lax.broadcasted_iota(jnp.int32, (1, LANES), 1)
    kmat_ref[:, MLA_KV_LORA:] = jnp.where(lane == _K_ONE - MLA_KV_LORA, 1.0, rot).astype(BF16)
    mlqk_ref[...] = proj(_C_MLQK)
    mlv_ref[...] = proj(_C_MLV).astype(BF16)
    mlo_ref[...] = proj(_C_MLO).astype(BF16)
    gates_ref[...] = proj(_C_GATES)
    hd = HG_HEADS * HG_DK
    hgp = proj(_C_HG)
    hgf_ref[...] = hgp[:, hd:2 * hd]
    hgr_ref[:, 0:hd] = hgp[:, 0:hd].astype(BF16)
    hgr_ref[:, hd:] = hgp[:, 2 * hd:].astype(BF16)
    cqn = _rms(cq, qn_ref[...]).astype(BF16)
    tab = tabq_ref[...]
    for h in range(MLA_HEADS):
        q_ref[h] = (_dot(cqn, wq_ref[:, h * QW:(h + 1) * QW]) * tab).astype(BF16)


def _inproj(x, g, w_perm, q_norm, kv_norm, wq, tabq, cos_t, sin_t, seq, layer, depth, stacks):
    n, d = x.shape
    tm = _token_tile(n)
    if tm <= seq:
        assert seq % tm == 0
        nt = seq // tm
        tab_map = lambda i: (i % nt, 0)
    else:
        assert tm % seq == 0 and tabq.shape[0] == tm
        tab_map = lambda i: (0, 0)
    row = lambda w: pl.BlockSpec((tm, w), lambda i: (i, 0))
    tab = lambda w: pl.BlockSpec((tm, w), tab_map)
    outs = [
        (jax.ShapeDtypeStruct((MLA_HEADS, n, QW), BF16), pl.BlockSpec((MLA_HEADS, tm, QW), lambda i: (0, i, 0))),
        (jax.ShapeDtypeStruct((n, QW), BF16), row(QW)),
        (jax.ShapeDtypeStruct((depth, n, MLA_KV_LORA), F32),
         pl.BlockSpec((1, tm, MLA_KV_LORA), lambda i: (layer, i, 0))),
        (jax.ShapeDtypeStruct((depth, n, MLA_ROPE), F32), pl.BlockSpec((1, tm, MLA_ROPE), lambda i: (layer, i, 0))),
        (jax.ShapeDtypeStruct((n, 512), F32), row(512)),
        (jax.ShapeDtypeStruct((n, 256), BF16), row(256)),
        (jax.ShapeDtypeStruct((n, 256), BF16), row(256)),
        (jax.ShapeDtypeStruct((n, LANES), F32), row(LANES)),
        (jax.ShapeDtypeStruct((n, 256), F32), row(256)),
        (jax.ShapeDtypeStruct((n, 768), BF16), row(768)),
    ]
    in_specs = [row(d), _const_spec((1, d)), _const_spec(w_perm.shape), _const_spec((1, MLA_Q_LORA)),
                _const_spec((1, MLA_KV_LORA)), _const_spec(wq.shape), tab(QW), tab(LANES), tab(LANES)]
    args = [x, g, w_perm, q_norm, kv_norm, wq, tabq, cos_t, sin_t]
    aliases = {}
    if stacks is not None:
        aliases = {len(args): 2, len(args) + 1: 3}
        in_specs += [pl.BlockSpec(memory_space=pl.ANY)] * 2
        args += list(stacks)
    return pl.pallas_call(
        _inproj_body,
        grid=(n // tm,),
        in_specs=in_specs,
        out_specs=[o[1] for o in outs],
        out_shape=[o[0] for o in outs],
        scratch_shapes=[pltpu.VMEM((tm, d), BF16)],
        input_output_aliases=aliases,
        compiler_params=_params(("parallel",)),
        name="in_proj",
    )(*args)


def _mla_prompt_body(q_ref, k_ref, o_ref, acc_ref, s_ref, *, tq, tk, nsub):
    i = pl.program_id(1)

    def keys(j):
        return k_ref[0, pl.ds(pl.multiple_of(j * tk, tk), tk), :]

    def consume(k, m_prev, nxt, shift):
        m_out = []
        if shift is not None:
            r = lax.broadcasted_iota(jnp.int32, (tq, tk), 0)
            c = lax.broadcasted_iota(jnp.int32, (tq, tk), 1)
            visible = (c // CHUNK) - (r // CHUNK) <= shift
        for h in range(MLA_HEADS):
            s = s_ref[h]
            if shift is not None:
                s = jnp.where(visible, s, NEG_BIG)
            m_new = jnp.maximum(m_prev[h], jnp.max(s, axis=-1, keepdims=True))
            alpha = jnp.exp2(m_prev[h] - m_new)
            p = jnp.exp2(s - m_new).astype(BF16)
            acc_ref[h] = alpha * acc_ref[h] + _dot(p, k)
            if nxt is not None:
                s_ref[h] = _dot_nt(q_ref[h, nxt[0] * tq:(nxt[0] + 1) * tq, :], nxt[1])
            m_out.append(m_new)
        return tuple(m_out)

    acc_ref[...] = jnp.zeros_like(acc_ref)
    k0 = keys(0)
    for h in range(MLA_HEADS):
        s_ref[h] = _dot_nt(q_ref[h, 0:tq, :], k0)
    m0 = tuple(jnp.full((tq, 1), NEG_BIG, F32) for _ in range(MLA_HEADS))
    for u in range(nsub):
        start = (i * nsub + u) * tq
        j_last = start // tk
        m1 = lax.fori_loop(0, j_last, lambda j, m: consume(keys(j), m, (u, keys(j + 1)), None), m0)
        consume(keys(j_last), m1, (u + 1, keys(0)) if u + 1 < nsub else None, (start - j_last * tk) // CHUNK)
        for h in range(MLA_HEADS):
            acc = acc_ref[h]
            o = acc[:, :MLA_KV_LORA] * (1.0 / acc[:, _K_ONE:_K_ONE + 1])
            o_ref[u * tq:(u + 1) * tq, h * MLA_KV_LORA:(h + 1) * MLA_KV_LORA] = o.astype(BF16)


def _mla_prompt(q, kmat, batch, seq):
    n = batch * seq
    tq = 256 if seq % 256 == 0 else seq
    tk = 2 * tq if seq % (2 * tq) == 0 else tq
    assert tq % CHUNK == 0 or tq == seq
    nsub = next(c for c in (8, 4, 2, 1) if seq % (c * tq) == 0)
    nq = seq // (nsub * tq)
    tqs = nsub * tq
    return pl.pallas_call(
        functools.partial(_mla_prompt_body, tq=tq, tk=tk, nsub=nsub),
        grid=(batch, nq),
        in_specs=[pl.BlockSpec((MLA_HEADS, tqs, QW), lambda b, i: (0, b * nq + i, 0)),
                  pl.BlockSpec((1, seq, QW), lambda b, i: (b, 0, 0))],
        out_specs=pl.BlockSpec((tqs, MLA_HEADS * MLA_KV_LORA), lambda b, i: (b * nq + i, 0)),
        out_shape=jax.ShapeDtypeStruct((n, MLA_HEADS * MLA_KV_LORA), BF16),
        scratch_shapes=[pltpu.VMEM((MLA_HEADS, tq, QW), F32), pltpu.VMEM((MLA_HEADS, tq, tk), F32)],
        compiler_params=_params(("parallel", "arbitrary")),
        name="mla_prompt",
    )(q, kmat.reshape(batch, seq, QW))


def _mla_sample_body(*refs, tq, has_bias):
    if has_bias:
        q_ref, ckvc_ref, kpec_ref, kn_ref, bc_ref, bn_ref, o_ref = refs
    else:
        q_ref, ckvc_ref, kpec_ref, kn_ref, o_ref = refs
    rows = MLA_HEADS * tq
    q = q_ref[...].reshape(rows, QW)
    ckv_c = ckvc_ref[0].astype(BF16)
    kpe_c = kpec_ref[0].astype(BF16)
    kn = kn_ref[...]
    r0, r1 = MLA_KV_LORA, MLA_KV_LORA + MLA_ROPE
    sc = _dot_nt(q[:, :r0], ckv_c) + _dot_nt(q[:, r0:r1], kpe_c) + _dot_nt(q[:, r1:r1 + MLA_ROPE], kpe_c)
    sn = _dot_nt(q, kn)
    if has_bias:
        sc = sc + jnp.concatenate([bc_ref[...]] * MLA_HEADS, axis=0)
        sn = sn + jnp.concatenate([bn_ref[...]] * MLA_HEADS, axis=0)
    m = jnp.maximum(jnp.max(sc, axis=-1, keepdims=True), jnp.max(sn, axis=-1, keepdims=True))
    pc = jnp.exp2(sc - m)
    pn = jnp.exp2(sn - m)
    l = jnp.sum(pc, axis=-1, keepdims=True) + jnp.sum(pn, axis=-1, keepdims=True)
    o = (_dot(pc.astype(BF16), ckv_c) + _dot(pn.astype(BF16), kn[:, :MLA_KV_LORA])) / l
    for h in range(MLA_HEADS):
        o_ref[:, h * MLA_KV_LORA:(h + 1) * MLA_KV_LORA] = o[h * tq:(h + 1) * tq].astype(BF16)


def _mla_sample(q, kmat_new, cache_ckv, cache_kpe, batch, seq):
    n = batch * seq
    past_len = cache_ckv.shape[1]
    q_pos = past_len + np.arange(seq)
    k_pos = np.arange(past_len + seq)
    mask = (k_pos[None, :] // CHUNK) <= (q_pos[:, None] // CHUNK)
    has_bias = not bool(mask.all())
    in_specs = [pl.BlockSpec((MLA_HEADS, seq, QW), lambda b: (0, b, 0)),
                pl.BlockSpec((1, past_len, MLA_KV_LORA), lambda b: (b, 0, 0)),
                pl.BlockSpec((1, past_len, MLA_ROPE), lambda b: (b, 0, 0)),
                pl.BlockSpec((seq, QW), lambda b: (b, 0))]
    args = [q, cache_ckv, cache_kpe, kmat_new]
    if has_bias:
        bias = np.where(mask, 0.0, NEG_BIG).astype(np.float32)
        in_specs += [_const_spec((seq, past_len)), _const_spec((seq, seq))]
        args += [jnp.asarray(bias[:, :past_len]), jnp.asarray(bias[:, past_len:])]
    return pl.pallas_call(
        functools.partial(_mla_sample_body, tq=seq, has_bias=has_bias),
        grid=(batch,),
        in_specs=in_specs,
        out_specs=pl.BlockSpec((seq, MLA_HEADS * MLA_KV_LORA), lambda b: (b, 0)),
        out_shape=jax.ShapeDtypeStruct((n, MLA_HEADS * MLA_KV_LORA), BF16),
        compiler_params=_params(("parallel",)),
        name="mla_sample",
    )(*args)


def _mlstm_body(qk_ref, v_ref, og_ref, gates_ref, cw_ref, cb_ref, gb_ref, c0_ref, n0_ref, m0_ref, conv0_ref,
                out_ref, c1_ref, n1_ref, m1_ref, conv1_ref,
                xbuf, qk_s, g_s, cbd, n_s, m_s, *, tb, chunk):
    j = pl.program_id(1)
    H, DK, DV = ML_HEADS, ML_DK, ML_DV
    HD = H * DK
    L = chunk
    pad = 8

    @pl.when(j == 0)
    def _init():
        xbuf[0:pad, :] = conv0_ref[0]
        cbd[...] = jnp.zeros_like(cbd)
        for h in range(H):
            cbd[h * DK:(h + 1) * DK, h * DV:(h + 1) * DV] = c0_ref[0, h]
        n_s[...] = n0_ref[0]
        m_s[...] = m0_ref[0]

    _mlstm_conv(qk_ref, cw_ref, cb_ref, xbuf, qk_s, tb, pad)
    gl = gates_ref[...] + gb_ref[...]
    lane_g = lax.broadcasted_iota(jnp.int32, (1, LANES), 1)
    g_s[...] = jnp.where(lane_g < H, gl, jnp.minimum(gl, 0.0) - jnp.log1p(jnp.exp(-jnp.abs(gl))))

    ri = lax.broadcasted_iota(jnp.int32, (L, L), 0)
    ci = lax.broadcasted_iota(jnp.int32, (L, L), 1)
    tri = (ci <= ri).astype(BF16)
    sel = (lax.broadcasted_iota(jnp.int32, (8, LANES), 0) == lax.broadcasted_iota(jnp.int32, (8, LANES), 1)).astype(BF16)
    row_l = lax.broadcasted_iota(jnp.int32, (L, H * L), 0)
    col_l = lax.broadcasted_iota(jnp.int32, (L, H * L), 1)
    causal = (col_l % L) <= row_l
    grp_d = _lane_group(HD, DK)
    rgrp = lax.broadcasted_iota(jnp.int32, (HD, 1), 0) // DK
    bd_mask = rgrp == _lane_group(H * DV, DV)

    def chunk_step(c, carry):
        r0 = pl.multiple_of(c * L, L)
        q = qk_s[pl.ds(r0, L), 0:HD]
        k = qk_s[pl.ds(r0, L), HD:2 * HD]
        v = v_ref[pl.ds(r0, L), :]
        g = g_s[pl.ds(r0, L), :]
        cs = _sel_dot(tri, g)
        x = jnp.where(lane_g < H, g, cs)
        xt = _sel_dot_nt(sel, x)
        li_c = [x[:, h:h + 1] for h in range(H)]
        b_c = [x[:, H + h:H + h + 1] for h in range(H)]
        m_prev = [m_s[:, h:h + 1] for h in range(H)]
        row_e = jnp.concatenate([xt[h:h + 1, :] - xt[H + h:H + h + 1, :] for h in range(H)], axis=1)
        logw = jnp.where(causal, _expand_cols(b_c, L) + row_e, -jnp.inf)
        m_intra = _seg_max(logw, L, H)
        log_inter = [b_c[h] + m_prev[h] for h in range(H)]
        m_t = [jnp.maximum(log_inter[h], m_intra[h]) for h in range(H)]
        w = jnp.exp(logw - _expand_cols(m_t, L))
        qb = q.astype(BF16)
        kexp = jnp.concatenate([jnp.where(grp_d == h, k, 0.0) for h in range(H)], axis=0).astype(BF16)
        vexp = jnp.concatenate([jnp.where(grp_d == h, v, 0.0) for h in range(H)], axis=0).astype(BF16)
        p = _dot_nt(qb, kexp) * w
        den_intra = _seg_sum(p, L, H)
        num = _dot(p.astype(BF16), vexp)
        w_inter = [jnp.exp(log_inter[h] - m_t[h]) for h in range(H)]
        num = num + _expand_cols(w_inter, DV) * _dot(qb, cbd[...].astype(BF16))
        qn = _seg_sum(q * n_s[...], DK, H)
        inv = [1.0 / jnp.maximum(jnp.abs(den_intra[h] + w_inter[h] * qn[h]), jnp.exp(-m_t[h])) for h in range(H)]
        hout = num * _expand_cols(inv, DV)
        og = og_ref[pl.ds(r0, L), :].astype(F32)
        out_ref[pl.ds(r0, L), :] = (jax.nn.sigmoid(og) * hout).astype(BF16)
        m_new = [m_t[h][L - 1:L, :] for h in range(H)]
        b_last = [b_c[h][L - 1:L, :] for h in range(H)]
        w_s = [jnp.exp(b_last[h] - b_c[h] + li_c[h] - m_new[h]) for h in range(H)]
        decay = [jnp.exp(b_last[h] + m_prev[h] - m_new[h]) for h in range(H)]
        kw = k * _expand_cols(w_s, DK)
        upd = _dot_tn(kw.astype(BF16), v.astype(BF16))
        dcol = decay[H - 1]
        for h in range(H - 2, -1, -1):
            dcol = jnp.where(rgrp == h, decay[h], dcol)
        cbd[...] = dcol * cbd[...] + jnp.where(bd_mask, upd, 0.0)
        n_s[...] = _expand_cols(decay, DK) * n_s[...] + jnp.sum(kw, axis=0, keepdims=True)
        m_row = m_s[...]
        for h in range(H):
            m_row = jnp.where(lane_g == h, m_new[h], m_row)
        m_s[...] = m_row
        return carry

    lax.fori_loop(0, tb // L, chunk_step, 0)

    @pl.when(j == pl.num_programs(1) - 1)
    def _fin():
        for h in range(H):
            c1_ref[0, h] = cbd[h * DK:(h + 1) * DK, h * DV:(h + 1) * DV]
        n1_ref[0] = n_s[...]
        m1_ref[0] = m_s[...]
        conv1_ref[0] = xbuf[0:pad, :]


def _mlstm_conv(qk_ref, cw_ref, cb_ref, xbuf, qk_s, tb, pad):
    HD = ML_HEADS * ML_DK
    xbuf[pad:pad + tb, :] = qk_ref[...]
    conv = cb_ref[...] + cw_ref[ML_CONV - 1:ML_CONV, :] * xbuf[pad:pad + tb, :]
    for t in range(1, ML_CONV):
        conv = conv + cw_ref[ML_CONV - 1 - t:ML_CONV - t, :] * xbuf[pad - t:pad - t + tb, :]
    act = conv * jax.nn.sigmoid(conv)
    lane = lax.broadcasted_iota(jnp.int32, (1, 2 * HD), 1)
    qk_s[...] = jnp.where(lane < HD, act, act * (ML_DK ** -0.5))
    xbuf[0:pad, :] = xbuf[tb:tb + pad, :]


def _mlstm_fast_body(qk_ref, v_ref, og_ref, gates_ref, cw_ref, cb_ref, gbc_ref, utri_ref, ecols_ref,
                     c0_ref, n0_ref, mc0_ref, mr0_ref, conv0_ref,
                     out_ref, c1_ref, n1_ref, m1_ref, conv1_ref,
                     xbuf, qk_s, y_s, u_s, cb_s, ks_s, dec_s, nr_s, num_s, den_s, wi_s, cbd, n_s, m_c, mrow_s, *, tb):
    j = pl.program_id(1)
    H, DK, DV = ML_HEADS, ML_DK, ML_DV
    HD = H * DK
    L = DK
    nc = tb // L
    pad = 8

    @pl.when(j == 0)
    def _init():
        xbuf[0:pad, :] = conv0_ref[0]
        cbd[...] = jnp.zeros_like(cbd)
        for h in range(H):
            cbd[h * DK:(h + 1) * DK, h * DV:(h + 1) * DV] = c0_ref[0, h]
        n_s[...] = n0_ref[0]
        m_c[...] = mc0_ref[0]
        mrow_s[...] = mr0_ref[0]

    g8 = gates_ref[...].T[0:8, :] + gbc_ref[...]
    lsig = jnp.minimum(g8, 0.0) - jnp.log1p(jnp.exp(-jnp.abs(g8)))
    lf8 = pltpu.roll(lsig, 8 - H, axis=0)
    fh, fm, fl = _split3(lf8)
    b8 = _dot(fh, utri_ref[...]) + _dot(fm, utri_ref[...]) + _dot(fl, utri_ref[...])
    a8 = g8 - b8
    lane_t = lax.broadcasted_iota(jnp.int32, (1, tb), 1)
    cmax = a8
    sh = 1
    while sh < tb:
        cmax = jnp.maximum(cmax, jnp.where(lane_t >= sh, pltpu.roll(cmax, sh, axis=1), -jnp.inf))
        sh *= 2
    mp8 = jnp.maximum(m_c[:, 0:1], cmax)
    mt8 = b8 + mp8
    valid = lax.broadcasted_iota(jnp.int32, (8, 1), 0) < H
    x32 = jnp.concatenate([jnp.where(valid, x, 0.0) for x in (mp8, a8, jnp.exp(-mt8), mt8)], axis=0)
    xh, xm, xl = _split3(x32)
    ec = ecols_ref[...]
    y_s[...] = _dot_tn(xh, ec) + _dot_tn(xm, ec) + _dot_tn(xl, ec)
    m_c[...] = jnp.broadcast_to(mt8[:, tb - 1:tb], m_c.shape)

    _mlstm_conv(qk_ref, cw_ref, cb_ref, xbuf, qk_s, tb, pad)

    grp = _lane_group(HD, DK)
    rgrp = lax.broadcasted_iota(jnp.int32, (HD, 1), 0) // DK
    bd_mask = rgrp == grp
    seg_b = bd_mask.astype(BF16)
    row_l = lax.broadcasted_iota(jnp.int32, (L, HD), 0)
    col_l = lax.broadcasted_iota(jnp.int32, (L, HD), 1) % L
    causal = col_l <= row_l
    diag = col_l == row_l

    for c in range(nc):
        r0 = c * L
        q = qk_s[r0:r0 + L, 0:HD]
        k = qk_s[r0:r0 + L, HD:2 * HD]
        vb = v_ref[r0:r0 + L, :].astype(BF16)
        m_e = y_s[r0:r0 + L, 0:HD]
        a_e = y_s[r0:r0 + L, HD:2 * HD]
        m_prev = mrow_s[...] if c == 0 else y_s[r0 - 1:r0, 0:HD]
        m_last = m_e[L - 1:L, :]
        row_a = jnp.sum(jnp.where(diag, a_e, 0.0), axis=0, keepdims=True)
        d = jnp.exp(jnp.where(causal, row_a - m_e, -jnp.inf))
        qb = q.astype(BF16)
        kb = k.astype(BF16)
        kexp = jnp.concatenate([jnp.where(grp == h, kb, jnp.zeros_like(kb)) for h in range(H)], axis=0)
        vexp = jnp.concatenate([jnp.where(grp == h, vb, jnp.zeros_like(vb)) for h in range(H)], axis=0)
        p = _dot_nt(qb, kexp) * d
        p_hi = p.astype(BF16)
        p_lo = (p - p_hi.astype(F32)).astype(BF16)
        num_s[r0:r0 + L, :] = _dot(p_hi, vexp)
        den_s[r0:r0 + L, :] = _dot(p_hi, seg_b) + _dot(p_lo, seg_b)
        wi_s[r0:r0 + L, :] = jnp.exp(m_prev - m_e)
        kw = k * jnp.exp(a_e - m_last)
        u_s[c] = jnp.where(bd_mask, _dot_tn(kw.astype(BF16), vb), 0.0)
        ks_s[c:c + 1, :] = jnp.sum(kw, axis=0, keepdims=True)
        dec_s[c:c + 1, :] = jnp.exp(m_prev - m_last)

    for c in range(nc):
        cb_s[c] = cbd[...].astype(BF16)
        nr_s[c:c + 1, :] = n_s[...]
        dec = dec_s[c:c + 1, :]
        cbd[...] = dec * cbd[...] + u_s[c]
        n_s[...] = dec * n_s[...] + ks_s[c:c + 1, :]

    for c in range(nc):
        r0 = c * L
        q = qk_s[r0:r0 + L, 0:HD]
        qn = q * nr_s[c:c + 1, :]
        qn_hi = qn.astype(BF16)
        qn_lo = (qn - qn_hi.astype(F32)).astype(BF16)
        w_inter = wi_s[r0:r0 + L, :]
        num = num_s[r0:r0 + L, :] + w_inter * _dot(q.astype(BF16), cb_s[c])
        den = den_s[r0:r0 + L, :] + w_inter * (_dot(qn_hi, seg_b) + _dot(qn_lo, seg_b))
        hout = num / jnp.maximum(jnp.abs(den), y_s[r0:r0 + L, 2 * HD:3 * HD])
        og = og_ref[r0:r0 + L, :].astype(F32)
        out_ref[r0:r0 + L, :] = (jax.nn.sigmoid(og) * hout).astype(BF16)

    mrow_s[...] = y_s[tb - 1:tb, 3 * HD:4 * HD]

    @pl.when(j == pl.num_programs(1) - 1)
    def _fin():
        for h in range(H):
            c1_ref[0, h] = cbd[h * DK:(h + 1) * DK, h * DV:(h + 1) * DV]
        n1_ref[0] = n_s[...]
        m1_ref[0] = m_c[...]
        conv1_ref[0] = xbuf[0:pad, :]


def _mlstm_fast(qk, v, og, gates, conv_w, conv_b, gate_bias, c0, n0, m0, conv0, batch, seq, tb):
    n = batch * seq
    H, DK, DV = ML_HEADS, ML_DK, ML_DV
    HD = H * DK
    nb = seq // tb
    nc = tb // DK
    pad = 8
    n0p = n0.reshape(batch, 1, HD)
    mc0 = jnp.broadcast_to(jnp.pad(m0, ((0, 0), (0, 8 - H)))[:, :, None], (batch, 8, LANES))
    mr0 = jnp.repeat(m0, DV, axis=1).reshape(batch, 1, HD)
    conv0p = jnp.pad(conv0, ((0, 0), (pad - (ML_CONV - 1), 0), (0, 0)))
    gbc = jnp.concatenate([gate_bias[0], gate_bias[1]])[:, None]
    idx = np.arange(tb)
    utri = jnp.asarray(idx[:, None] <= idx[None, :], BF16)
    ecols_np = np.zeros((32, 4 * HD), np.float32)
    for kq in range(4):
        for h in range(H):
            ecols_np[8 * kq + h, kq * HD + h * DV:kq * HD + (h + 1) * DV] = 1.0
    ecols = jnp.asarray(ecols_np, BF16)
    row = lambda w: pl.BlockSpec((tb, w), lambda b, j: (b * nb + j, 0))
    per_b = lambda shape: pl.BlockSpec((1,) + shape, lambda b, j: (b,) + (0,) * len(shape))
    f32 = lambda *shape: pltpu.VMEM(shape, F32)
    out, c1, n1, m1, conv1 = pl.pallas_call(
        functools.partial(_mlstm_fast_body, tb=tb),
        grid=(batch, nb),
        in_specs=[row(2 * HD), row(H * DV), row(H * DV), row(LANES),
                  _const_spec((ML_CONV, 2 * HD)), _const_spec((1, 2 * HD)), _const_spec((2 * H, 1)),
                  _const_spec((tb, tb)), _const_spec((32, 4 * HD)),
                  per_b((H, DK, DV)), per_b((1, HD)), per_b((8, LANES)), per_b((1, HD)), per_b((pad, 2 * HD))],
        out_specs=[row(H * DV), per_b((H, DK, DV)), per_b((1, HD)), per_b((8, LANES)), per_b((pad, 2 * HD))],
        out_shape=[jax.ShapeDtypeStruct((n, H * DV), BF16), jax.ShapeDtypeStruct((batch, H, DK, DV), F32),
                   jax.ShapeDtypeStruct((batch, 1, HD), F32), jax.ShapeDtypeStruct((batch, 8, LANES), F32),
                   jax.ShapeDtypeStruct((batch, pad, 2 * HD), F32)],
        scratch_shapes=[f32(tb + pad, 2 * HD), f32(tb, 2 * HD), f32(tb, 4 * HD), f32(nc, HD, H * DV),
                        pltpu.VMEM((nc, HD, H * DV), BF16), f32(nc, HD), f32(nc, HD), f32(nc, HD),
                        f32(tb, HD), f32(tb, HD), f32(tb, HD), f32(HD, H * DV), f32(1, HD), f32(8, LANES),
                        f32(1, HD)],
        compiler_params=_params(("parallel", "arbitrary")),
        name="mlstm_blocked",
    )(qk, v, og, gates, conv_w, conv_b, gbc, utri, ecols, c0, n0p, mc0, mr0, conv0p)
    return (out, c1, n1.reshape(batch, H, DK), m1[:, :H, 0], conv1[:, pad - (ML_CONV - 1):, :])


def _mlstm(qk, v, og, gates, conv_w, conv_b, gate_bias, c0, n0, m0, conv0, batch, seq):
    if ML_DK == ML_DV == CHUNK and seq % 512 == 0:
        return _mlstm_fast(qk, v, og, gates, conv_w, conv_b, gate_bias, c0, n0, m0, conv0, batch, seq, 512)
    gb = gate_bias
    gbias = jnp.concatenate([gb[0], gb[1], jnp.zeros((LANES - 2 * ML_HEADS,), F32)])[None, :]
    n = batch * seq
    H, DK, DV = ML_HEADS, ML_DK, ML_DV
    chunk = min(CHUNK, seq)
    assert seq % chunk == 0 and seq >= ML_CONV - 1 and chunk % 8 == 0
    tb = 512 if seq % 512 == 0 else chunk
    nb = seq // tb
    pad = 8
    n0p = n0.reshape(batch, 1, H * DK)
    m0p = jnp.pad(m0.reshape(batch, 1, H), ((0, 0), (0, 0), (0, LANES - H)))
    conv0p = jnp.pad(conv0, ((0, 0), (pad - (ML_CONV - 1), 0), (0, 0)))
    row = lambda w: pl.BlockSpec((tb, w), lambda b, j: (b * nb + j, 0))
    per_b = lambda shape: pl.BlockSpec((1,) + shape, lambda b, j: (b,) + (0,) * len(shape))
    out, c1, n1, m1, conv1 = pl.pallas_call(
        functools.partial(_mlstm_body, tb=tb, chunk=chunk),
        grid=(batch, nb),
        in_specs=[row(2 * H * DK), row(H * DV), row(H * DV), row(LANES),
                  _const_spec((ML_CONV, 2 * H * DK)), _const_spec((1, 2 * H * DK)), _const_spec((1, LANES)),
                  per_b((H, DK, DV)), per_b((1, H * DK)), per_b((1, LANES)), per_b((pad, 2 * H * DK))],
        out_specs=[row(H * DV), per_b((H, DK, DV)), per_b((1, H * DK)), per_b((1, LANES)), per_b((pad, 2 * H * DK))],
        out_shape=[jax.ShapeDtypeStruct((n, H * DV), BF16), jax.ShapeDtypeStruct((batch, H, DK, DV), F32),
                   jax.ShapeDtypeStruct((batch, 1, H * DK), F32), jax.ShapeDtypeStruct((batch, 1, LANES), F32),
                   jax.ShapeDtypeStruct((batch, pad, 2 * H * DK), F32)],
        scratch_shapes=[pltpu.VMEM((tb + pad, 2 * H * DK), F32), pltpu.VMEM((tb, 2 * H * DK), F32),
                        pltpu.VMEM((tb, LANES), F32), pltpu.VMEM((H * DK, H * DV), F32),
                        pltpu.VMEM((1, H * DK), F32), pltpu.VMEM((1, LANES), F32)],
        compiler_params=_params(("parallel", "arbitrary")),
        name="mlstm",
    )(qk, v, og, gates, conv_w, conv_b, gbias, c0, n0p, m0p, conv0p)
    return (out, c1, n1.reshape(batch, H, DK), m1[:, 0, :H], conv1[:, pad - (ML_CONV - 1):, :])


def _hgrn_body(hgf_ref, hgr_ref, lbraw_ref, hnorm_ref, tri_ref, s0_ref, out_ref, s1_ref,
               hq_s, kk_s, f_s, b_s, qe_s, oi_s, u_s, sts_s, dec_s, st, *, tb, sub, layer):
    j = pl.program_id(1)
    H, DK, DV = HG_HEADS, HG_DK, HG_DV
    HD = H * DK
    S = sub

    @pl.when(j == 0)
    def _init():
        st[...] = jnp.zeros_like(st)
        for h in range(H):
            st[h * DV:(h + 1) * DV, h * DK:(h + 1) * DK] = s0_ref[0, h].T

    raw = lbraw_ref[...]
    e = jnp.exp(raw - jnp.max(raw, axis=0, keepdims=True))
    sm = e / jnp.sum(e, axis=0, keepdims=True)
    cum = sm[0:1, :]
    for l in range(1, layer + 1):
        cum = cum + sm[l:l + 1, :]
    lb = cum - sm[0:1, :]

    z = hgf_ref[...]
    lsig = jnp.minimum(z, 0.0) - jnp.log1p(jnp.exp(-jnp.abs(z)))
    a1 = jnp.log(lb)
    a2 = jnp.log1p(-lb) + lsig
    delta = a1 - a2
    lf = jnp.where(jnp.isnan(delta), a1 + a2, jnp.maximum(a1, a2) + jnp.log1p(jnp.exp(-jnp.abs(delta))))
    f_s[...] = jnp.exp(lf)
    b_all = _sel_dot(tri_ref[...], lf)
    b_s[...] = b_all
    kk_s[...] = (1.0 - lb) * jax.nn.sigmoid(-z)
    xq = hgr_ref[:, 0:HD].astype(F32)
    hq_all = xq * jax.nn.sigmoid(xq)
    hq_s[...] = hq_all
    qe_s[...] = (hq_all * jnp.exp(b_all)).astype(BF16)

    rowi = lax.broadcasted_iota(jnp.int32, (S, 1), 0)
    seg = ((lax.broadcasted_iota(jnp.int32, (HD, 1), 0) // DK) == _lane_group(H * DV, DV))
    seg_b = seg.astype(BF16)
    gnorm = hnorm_ref[...]

    n_sub = tb // S

    for c in range(n_sub):
        r0 = c * S
        hq = hq_s[r0:r0 + S, :]
        kk = kk_s[r0:r0 + S, :]
        f = f_s[r0:r0 + S, :]
        b = b_s[r0:r0 + S, :]
        iv_b = hgr_ref[r0:r0 + S, HD:HD + H * DV]
        iv = iv_b.astype(F32)
        bl = b[S - 1:S, :]
        kd = kk * jnp.exp(bl - b)
        u_s[c] = jnp.where(seg, _dot_tn(iv_b, kd.astype(BF16)), 0.0)
        dec_s[c:c + 1, :] = jnp.exp(bl)
        qd = jnp.where(rowi == S - 1, hq, 0.0)
        ws = [None] * S
        ws[S - 1] = (qd * kk[S - 1:S, :]).astype(BF16)
        for s in range(S - 2, -1, -1):
            qd = jnp.where(rowi == s, hq, qd * f[s + 1:s + 2, :])
            ws[s] = (qd * kk[s:s + 1, :]).astype(BF16)
        r = _dot(jnp.concatenate(ws, axis=0), seg_b)
        oi = r[0:S, :] * iv[0:1, :]
        for s in range(1, S):
            oi = oi + r[s * S:(s + 1) * S, :] * iv[s:s + 1, :]
        oi_s[r0:r0 + S, :] = oi

    for c in range(n_sub):
        sts_s[c] = st[...].astype(BF16)
        st[...] = dec_s[c:c + 1, :] * st[...] + u_s[c]

    grp_rows = min(n_sub, 4) * S
    for c0 in range(0, n_sub, grp_rows // S):
        r0 = c0 * S
        o = jnp.concatenate([oi_s[(c0 + i) * S:(c0 + i + 1) * S, :]
                             + _dot_nt(qe_s[(c0 + i) * S:(c0 + i + 1) * S, :], sts_s[c0 + i])
                             for i in range(grp_rows // S)], axis=0)
        gt = hgr_ref[r0:r0 + grp_rows, HD + H * DV:HD + 2 * H * DV].astype(F32)
        ms = _dot((o * o).astype(BF16), seg_b) * (1.0 / DV)
        y = o * lax.rsqrt(ms + EPS) * gnorm * (gt * jax.nn.sigmoid(gt))
        out_ref[r0:r0 + grp_rows, :] = y.astype(BF16)

    @pl.when(j == pl.num_programs(1) - 1)
    def _fin():
        for h in range(H):
            s1_ref[0, h] = st[h * DV:(h + 1) * DV, h * DK:(h + 1) * DK].T


def _hgrn(hg_f, hg_r, lb_raw, hnorm, s0, batch, seq, layer):
    n = batch * seq
    H, DK, DV = HG_HEADS, HG_DK, HG_DV
    assert DK == DV
    sub = min(HG_SUB, seq)
    assert seq % sub == 0 and sub % 8 == 0
    tb = 512 if seq % 512 == 0 else sub
    nb = seq // tb
    depth = lb_raw.shape[0]
    n_sub = tb // sub
    idx = np.arange(tb)
    tri = jnp.asarray((idx[:, None] // sub == idx[None, :] // sub) & (idx[None, :] <= idx[:, None]), BF16)
    per_b = pl.BlockSpec((1, H, DK, DV), lambda b, j: (b, 0, 0, 0))
    out, s1 = pl.pallas_call(
        functools.partial(_hgrn_body, tb=tb, sub=sub, layer=layer),
        grid=(batch, nb),
        in_specs=[pl.BlockSpec((tb, H * DK), lambda b, j: (b * nb + j, 0)),
                  pl.BlockSpec((tb, H * DK + 2 * H * DV), lambda b, j: (b * nb + j, 0)),
                  _const_spec((depth, H * DK)), _const_spec((1, H * DV)), _const_spec((tb, tb)), per_b],
        out_specs=[pl.BlockSpec((tb, H * DV), lambda b, j: (b * nb + j, 0)), per_b],
        out_shape=[jax.ShapeDtypeStruct((n, H * DV), BF16), jax.ShapeDtypeStruct((batch, H, DK, DV), F32)],
        scratch_shapes=[pltpu.VMEM((tb, H * DK), F32), pltpu.VMEM((tb, H * DK), F32), pltpu.VMEM((tb, H * DK), F32),
                        pltpu.VMEM((tb, H * DK), F32), pltpu.VMEM((tb, H * DK), BF16),
                        pltpu.VMEM((tb, H * DV), F32), pltpu.VMEM((n_sub, H * DV, H * DK), F32),
                        pltpu.VMEM((n_sub, H * DV, H * DK), BF16), pltpu.VMEM((max(n_sub, 8), H * DK), F32),
                        pltpu.VMEM((H * DV, H * DK), F32)],
        compiler_params=_params(("parallel", "arbitrary")),
        name="hgrn2",
    )(hg_f, hg_r, lb_raw, hnorm, tri, s0)
    return out, s1


def _outproj_ffn_body(x_ref, o_ref, ml_ref, hgo_ref, g_ref, w1_ref, w2_ref, w3_ref,
                      gpre_ref, gpost_ref, wg_ref, wu_ref, wo_ref, y_ref, xn_ref, acc_ref, *, fc):
    mix = _dot(o_ref[...], w1_ref[...]) + _dot(ml_ref[...], w2_ref[...]) + _dot(hgo_ref[...], w3_ref[...])
    x1 = x_ref[...] + _rms(mix, g_ref[...])
    y_ref[...] = _ffn_apply(x1, gpre_ref, gpost_ref, wg_ref, wu_ref, wo_ref, xn_ref, acc_ref, fc)


def _outproj_ffn(x, o_lat, ml_out, hg_out, g, w1, w2, w3, g_pre, g_post, w_gate, w_up, w_out):
    n, d = x.shape
    d_ff = w_gate.shape[1]
    tm = _token_tile(n)
    fc = 256 if d_ff % 256 == 0 else d_ff
    row = lambda w: pl.BlockSpec((tm, w), lambda i: (i, 0))
    return pl.pallas_call(
        functools.partial(_outproj_ffn_body, fc=fc),
        grid=(n // tm,),
        in_specs=[row(d), row(o_lat.shape[1]), row(ml_out.shape[1]), row(hg_out.shape[1]), _const_spec((1, d)),
                  _const_spec(w1.shape), _const_spec(w2.shape), _const_spec(w3.shape),
                  _const_spec((1, d)), _const_spec((1, d)), _const_spec((d, d_ff)), _const_spec((d, d_ff)),
                  _const_spec((d_ff, d))],
        out_specs=row(d),
        out_shape=jax.ShapeDtypeStruct((n, d), F32),
        scratch_shapes=[pltpu.VMEM((tm, d), BF16), pltpu.VMEM((tm, d), F32)],
        compiler_params=_params(("parallel",)),
        name="out_proj_ffn",
    )(x, o_lat, ml_out, hg_out, g, w1, w2, w3, g_pre, g_post, w_gate, w_up, w_out)


def _rope_tables(pos):
    inv = ROPE_THETA ** (-jnp.arange(0, MLA_ROPE, 2, dtype=F32) / MLA_ROPE)
    ang = pos.astype(F32)[:, None] * inv[None, :]
    cos, sin = jnp.cos(ang), jnp.sin(ang)
    t = pos.shape[0]
    cos2, sin2 = jnp.concatenate([cos, cos], -1), jnp.concatenate([sin, sin], -1)
    tabq = (MLA_SCALE * LOG2E) * jnp.concatenate([jnp.ones((t, MLA_KV_LORA), F32), cos2, sin2,
                                        jnp.zeros((t, QW - MLA_KV_LORA - 2 * MLA_ROPE), F32)], -1)
    zpad = jnp.zeros((t, LANES - 2 * MLA_ROPE), F32)
    cos_t = jnp.concatenate([cos2, cos2, zpad], -1)
    sin_t = jnp.concatenate([sin2, sin2, zpad], -1)
    return tabq, cos_t, sin_t


def _swap_halves(w):
    half = w.shape[-1] // 2
    return jnp.concatenate([-w[..., half:], w[..., :half]], axis=-1)


def _layer_weights(l, ln_gains, w_ffn_in, w_ffn_out, w_in, w_out, mla_q_norm, mla_kv_norm, mla_w_uq, mla_w_uk,
                   mla_w_uv, ml_conv_w, ml_conv_b, ml_gate_bias, hg_norm):
    d = w_in.shape[1]
    d_ff = w_ffn_out.shape[2]
    sizes = (MLA_Q_LORA, MLA_KV_LORA, MLA_ROPE, 2 * ML_HEADS * ML_DK, ML_HEADS * ML_DV, ML_HEADS * ML_DV, ML_HEADS,
             ML_HEADS, HG_HEADS * HG_DK, HG_HEADS * HG_DK, HG_HEADS * HG_DV, HG_HEADS * HG_DV)
    pts = np.cumsum((0,) + sizes)
    part = [w_in[l][:, pts[i]:pts[i + 1]] for i in range(len(sizes))]
    zeros = lambda n: jnp.zeros((d, n), F32)
    kpe, kpe_sw = part[2], _swap_halves(part[2])
    w_perm = jnp.concatenate(
        [part[0], part[1], kpe, kpe, zeros(LANES - 2 * MLA_ROPE), kpe_sw, kpe_sw, zeros(LANES - 2 * MLA_ROPE),
         part[3], part[4], part[5], part[6], part[7], zeros(LANES - 2 * ML_HEADS), part[8], part[9], part[10],
         part[11]], axis=1).astype(BF16)
    assert w_perm.shape[1] == _IN_COLS
    uq = mla_w_uq[l].reshape(MLA_Q_LORA, MLA_HEADS, MLA_NOPE + MLA_ROPE)
    uq_nope = jnp.transpose(uq[:, :, :MLA_NOPE], (1, 0, 2))
    uk_t = jnp.transpose(mla_w_uk[l], (1, 2, 0))
    w_lat = _bmm(uq_nope, uk_t)
    uq_pe = jnp.transpose(uq[:, :, MLA_NOPE:], (1, 0, 2))
    wq = jnp.concatenate([w_lat, uq_pe, _swap_halves(uq_pe),
                          jnp.zeros((MLA_HEADS, MLA_Q_LORA, QW - MLA_KV_LORA - 2 * MLA_ROPE), F32)], axis=-1)
    wq = jnp.transpose(wq, (1, 0, 2)).reshape(MLA_Q_LORA, MLA_HEADS * QW).astype(BF16)
    n_mla = MLA_HEADS * MLA_V
    uv = jnp.transpose(mla_w_uv[l], (1, 0, 2))
    wo_mla = w_out[l][:n_mla].reshape(MLA_HEADS, MLA_V, d)
    w1 = _bmm(uv, wo_mla).reshape(MLA_HEADS * MLA_KV_LORA, d).astype(BF16)
    n_ml = ML_HEADS * ML_DV
    w2 = w_out[l][n_mla:n_mla + n_ml].astype(BF16)
    w3 = w_out[l][n_mla + n_ml:].astype(BF16)
    ffn = []
    for j in range(2):
        wi = w_ffn_in[l, j]
        ffn.append((wi[:, :d_ff].astype(BF16), wi[:, d_ff:].astype(BF16), w_ffn_out[l, j].astype(BF16)))
    return dict(ln=ln_gains[l], ffn=ffn, w_perm=w_perm, wq=wq, w1=w1, w2=w2, w3=w3,
                q_norm=mla_q_norm[l][None, :], kv_norm=mla_kv_norm[l][None, :], conv_w=ml_conv_w[l],
                conv_b=ml_conv_b[l][None, :], gate_bias=ml_gate_bias[l].astype(F32), hnorm=jnp.tile(hg_norm[l], HG_HEADS)[None, :])


def _tile_rows(t, reps):
    return jnp.tile(t, (reps, 1)) if reps > 1 else t


def _layer(x, lw, layer, depth, lb_raw, batch, seq, pos, past, stacks):
    n, d = x.shape
    g = lw['ln']
    gain = lambda i: g[i][None, :]
    x = _ffn(x, gain(0), gain(1), *lw['ffn'][0])
    tabq, cos_t, sin_t = _rope_tables(pos)
    tm = _token_tile(n)
    reps = tm // seq if seq < tm else 1
    tabs = [_tile_rows(t, reps) for t in (tabq, cos_t, sin_t)]
    q, kmat, ckv, kpe, mlqk, mlv, mlo, gates, hg_f, hg_r = _inproj(x, gain(2), lw['w_perm'], lw['q_norm'], lw['kv_norm'],
                                                          lw['wq'], *tabs, seq, layer, depth, stacks)
    if past is None:
        o_lat = _mla_prompt(q, kmat, batch, seq)
        c0 = jnp.zeros((batch, ML_HEADS, ML_DK, ML_DV), F32)
        n0 = jnp.zeros((batch, ML_HEADS, ML_DK), F32)
        m0 = jnp.zeros((batch, ML_HEADS), F32)
        conv0 = jnp.zeros((batch, ML_CONV - 1, 2 * ML_HEADS * ML_DK), F32)
        s0 = jnp.zeros((batch, HG_HEADS, HG_DK, HG_DV), F32)
    else:
        cache_ckv, cache_kpe, c0, n0, m0, conv0, s0 = past
        o_lat = _mla_sample(q, kmat, cache_ckv, cache_kpe, batch, seq)
    ml_out, c1, n1, m1, conv1 = _mlstm(mlqk, mlv, mlo, gates, lw['conv_w'], lw['conv_b'], lw['gate_bias'],
                                       c0, n0, m0, conv0, batch, seq)
    hg_out, s1 = _hgrn(hg_f, hg_r, lb_raw, lw['hnorm'], s0, batch, seq, layer)
    x = _outproj_ffn(x, o_lat, ml_out, hg_out, gain(3), lw['w1'], lw['w2'], lw['w3'],
                     gain(4), gain(5), *lw['ffn'][1])
    return x, (ckv, kpe), (c1, n1, m1, conv1, s1)


def kernel(x_prompt, x_sample, cache_ckv, cache_kpe, state_mlstm_c, state_mlstm_n, state_mlstm_m, state_mlstm_conv, state_hgrn, ln_gains, w_ffn_in, w_ffn_out, w_in, w_out, mla_q_norm, mla_kv_norm, mla_w_uq, mla_w_uk, mla_w_uv, ml_conv_w, ml_conv_b, ml_gate_bias, hg_lb_raw, hg_norm):
    depth = w_in.shape[0]
    bp, tp, d = x_prompt.shape
    bs, ts, _ = x_sample.shape
    past_len = cache_ckv.shape[2]
    pos_p = jnp.arange(tp)
    pos_s = past_len + jnp.arange(ts)
    yp = x_prompt.reshape(bp * tp, d)
    ys = x_sample.reshape(bs * ts, d)
    lb_raw = hg_lb_raw.astype(F32)
    p_states, s_states = [], []
    p_kv = s_kv = None
    for l in range(depth):
        lw = _layer_weights(l, ln_gains, w_ffn_in, w_ffn_out, w_in, w_out, mla_q_norm, mla_kv_norm, mla_w_uq,
                            mla_w_uk, mla_w_uv, ml_conv_w, ml_conv_b, ml_gate_bias, hg_norm)
        yp, p_kv, st_p = _layer(yp, lw, l, depth, lb_raw, bp, tp, pos_p, None, p_kv)
        past = (cache_ckv[l], cache_kpe[l], state_mlstm_c[l], state_mlstm_n[l], state_mlstm_m[l],
                state_mlstm_conv[l], state_hgrn[l])
        ys, s_kv, st_s = _layer(ys, lw, l, depth, lb_raw, bs, ts, pos_s, past, s_kv)
        p_states.append(st_p)
        s_states.append(st_s)
    p_out = [jnp.stack([st[i] for st in p_states]) for i in range(5)]
    s_out = [jnp.stack([st[i] for st in s_states]) for i in range(5)]
    return (yp.reshape(bp, tp, d), ys.reshape(bs, ts, d),
            p_kv[0].reshape(depth, bp, tp, MLA_KV_LORA), p_kv[1].reshape(depth, bp, tp, MLA_ROPE), *p_out,
            s_kv[0].reshape(depth, bs, ts, MLA_KV_LORA), s_kv[1].reshape(depth, bs, ts, MLA_ROPE), *s_out)
```

```python
import functools

import numpy as np
import jax
import jax.numpy as jnp
from jax import lax
from jax.experimental import pallas as pl
from jax.experimental.pallas import tpu as pltpu

F32 = jnp.float32
BF16 = jnp.bfloat16

CHUNK = 64
FFN_RES = 0.5
EPS = 1e-6
MLA_HEADS = 8
MLA_NOPE = 64
MLA_ROPE = 32
MLA_V = 64
MLA_Q_LORA = 256
MLA_KV_LORA = 128
ROPE_THETA = 10000.0
MLA_SCALE = (MLA_NOPE + MLA_ROPE) ** -0.5
ML_HEADS = 4
ML_DK = 64
ML_DV = 64
ML_CONV = 4
HG_HEADS = 4
HG_DK = 64
HG_DV = 64

LANES = 128
VMEM_LIMIT = 56 * 1024 * 1024
QW = 2 * LANES
_K_ONE = MLA_KV_LORA + 2 * MLA_ROPE
LOG2E = 1.4426950408889634
HG_SUB = 16
NEG_BIG = -1e30
_F32_BIG = 3.0e38


def _rms(x, g):
    return x * lax.rsqrt(jnp.mean(x * x, axis=-1, keepdims=True) + EPS) * g


def _dot(a, b):
    return jnp.dot(a, b, preferred_element_type=F32)


def _dot_nt(a, b):
    return lax.dot_general(a, b, (((1,), (1,)), ((), ())), preferred_element_type=F32)


def _dot_tn(a, b):
    return lax.dot_general(a, b, (((0,), (0,)), ((), ())), preferred_element_type=F32)


def _split3(x):
    hi = x.astype(BF16)
    r = x - hi.astype(F32)
    mid = r.astype(BF16)
    lo = (r - mid.astype(F32)).astype(BF16)
    return hi, mid, lo


def _sel_dot(sel, x):
    hi, mid, lo = _split3(x)
    return _dot(sel, hi) + _dot(sel, mid) + _dot(sel, lo)


def _sel_dot_nt(sel, x):
    hi, mid, lo = _split3(x)
    return _dot_nt(sel, hi) + _dot_nt(sel, mid) + _dot_nt(sel, lo)


def _lane_group(n, g):
    return lax.broadcasted_iota(jnp.int32, (1, n), 1) // g


def _expand_cols(cols, g):
    grp = _lane_group(len(cols) * g, g)
    out = cols[-1]
    for h in range(len(cols) - 2, -1, -1):
        out = jnp.where(grp == h, cols[h], out)
    return out


def _seg_sum(x, g, heads):
    grp = _lane_group(heads * g, g)
    return [jnp.sum(jnp.where(grp == h, x, 0.0), axis=-1, keepdims=True) for h in range(heads)]


def _seg_max(x, g, heads):
    grp = _lane_group(heads * g, g)
    return [jnp.max(jnp.where(grp == h, x, -jnp.inf), axis=-1, keepdims=True) for h in range(heads)]


def _const_spec(shape):
    nd = len(shape)
    return pl.BlockSpec(shape, lambda *_: (0,) * nd, pipeline_mode=pl.Buffered(1))


def _params(sem):
    return pltpu.CompilerParams(dimension_semantics=sem, vmem_limit_bytes=VMEM_LIMIT)


def _token_tile(n):
    for t in (512, 256, 128, 64, 32, 16, 8):
        if n % t == 0:
            return t
    raise ValueError(f"token count {n} must be a multiple of 8")


def _bmm_body(a_ref, b_ref, o_ref):
    o_ref[0] = jnp.dot(a_ref[0], b_ref[0], preferred_element_type=F32, precision=lax.Precision.HIGHEST)


def _bmm(a, b):
    h, m, k = a.shape
    n = b.shape[2]
    return pl.pallas_call(
        _bmm_body,
        grid=(h,),
        in_specs=[pl.BlockSpec((1, m, k), lambda i: (i, 0, 0)), pl.BlockSpec((1, k, n), lambda i: (i, 0, 0))],
        out_specs=pl.BlockSpec((1, m, n), lambda i: (i, 0, 0)),
        out_shape=jax.ShapeDtypeStruct((h, m, n), F32),
        compiler_params=_params(("parallel",)),
        name="weight_fold",
    )(a, b)


def _ffn_apply(x, gpre_ref, gpost_ref, wg_ref, wu_ref, wo_ref, xn_ref, acc_ref, fc):
    xn_ref[...] = _rms(x, gpre_ref[...]).astype(BF16)
    d_ff = wg_ref.shape[1]
    for c in range(d_ff // fc):
        xn = xn_ref[...]
        sl = slice(c * fc, (c + 1) * fc)
        gate = _dot(xn, wg_ref[:, sl])
        up = _dot(xn, wu_ref[:, sl])
        act = (gate * jax.nn.sigmoid(gate) * up).astype(BF16)
        part = _dot(act, wo_ref[sl, :])
        if c == 0:
            acc_ref[...] = part
        else:
            acc_ref[...] += part
    return x + FFN_RES * _rms(acc_ref[...], gpost_ref[...])


def _ffn_body(x_ref, gpre_ref, gpost_ref, wg_ref, wu_ref, wo_ref, o_ref, xn_ref, acc_ref, *, fc):
    o_ref[...] = _ffn_apply(x_ref[...], gpre_ref, gpost_ref, wg_ref, wu_ref, wo_ref, xn_ref, acc_ref, fc)


def _ffn(x, g_pre, g_post, w_gate, w_up, w_out):
    n, d = x.shape
    d_ff = w_gate.shape[1]
    tm = _token_tile(n)
    fc = 256 if d_ff % 256 == 0 else d_ff
    row = pl.BlockSpec((tm, d), lambda i: (i, 0))
    return pl.pallas_call(
        functools.partial(_ffn_body, fc=fc),
        grid=(n // tm,),
        in_specs=[row, _const_spec((1, d)), _const_spec((1, d)), _const_spec((d, d_ff)), _const_spec((d, d_ff)),
                  _const_spec((d_ff, d))],
        out_specs=row,
        out_shape=jax.ShapeDtypeStruct((n, d), F32),
        scratch_shapes=[pltpu.VMEM((tm, d), BF16), pltpu.VMEM((tm, d), F32)],
        compiler_params=_params(("parallel",)),
        name="ffn",
    )(x, g_pre, g_post, w_gate, w_up, w_out)


_C_CQ = (0, 256)
_C_CKV = (256, 384)
_C_ROPE_A = (384, 512)
_C_ROPE_B = (512, 640)
_C_MLQK = (640, 1152)
_C_MLV = (1152, 1408)
_C_MLO = (1408, 1664)
_C_GATES = (1664, 1792)
_C_HG = (1792, 2816)
_IN_COLS = 2816


def _inproj_body(x_ref, g_ref, w_ref, qn_ref, kvn_ref, wq_ref, tabq_ref, cos_ref, sin_ref, *rest):
    q_ref, kmat_ref, ckv_ref, kpe_ref, mlqk_ref, mlv_ref, mlo_ref, gates_ref, hgf_ref, hgr_ref, hn_ref = rest[-11:]
    hn_ref[...] = _rms(x_ref[...], g_ref[...]).astype(BF16)

    def proj(cols):
        return _dot(hn_ref[...], w_ref[:, cols[0]:cols[1]])

    cq = proj(_C_CQ)
    ckv = _rms(proj(_C_CKV), kvn_ref[...])
    ckv_ref[0] = ckv
    rot = proj(_C_ROPE_A) * cos_ref[...] + proj(_C_ROPE_B) * sin_ref[...]
    kpe_ref[0] = rot[:, :MLA_ROPE]
    kmat_ref[:, :MLA_KV_LORA] = ckv.astype(BF16)
    lane = lax.broadcasted_iota(jnp.int32, (1, LANES), 1)
    kmat_ref[:, MLA_KV_LORA:] = jnp.where(lane == _K_ONE - MLA_KV_LORA, 1.0, rot).astype(BF16)
    mlqk_ref[...] = proj(_C_MLQK)
    mlv_ref[...] = proj(_C_MLV).astype(BF16)
    mlo_ref[...] = proj(_C_MLO).astype(BF16)
    gates_ref[...] = proj(_C_GATES)
    hd = HG_HEADS * HG_DK
    hgp = proj(_C_HG)
    hgf_ref[...] = hgp[:, hd:2 * hd]
    hgr_ref[:, 0:hd] = hgp[:, 0:hd].astype(BF16)
    hgr_ref[:, hd:] = hgp[:, 2 * hd:].astype(BF16)
    cqn = _rms(cq, qn_ref[...]).astype(BF16)
    tab = tabq_ref[...]
    for h in range(MLA_HEADS):
        q_ref[h] = (_dot(cqn, wq_ref[:, h * QW:(h + 1) * QW]) * tab).astype(BF16)


def _inproj(x, g, w_perm, q_norm, kv_norm, wq, tabq, cos_t, sin_t, seq, layer, depth, stacks):
    n, d = x.shape
    tm = _token_tile(n)
    if tm <= seq:
        assert seq % tm == 0
        nt = seq // tm
        tab_map = lambda i: (i % nt, 0)
    else:
        assert tm % seq == 0 and tabq.shape[0] == tm
        tab_map = lambda i: (0, 0)
    row = lambda w: pl.BlockSpec((tm, w), lambda i: (i, 0))
    tab = lambda w: pl.BlockSpec((tm, w), tab_map)
    outs = [
        (jax.ShapeDtypeStruct((MLA_HEADS, n, QW), BF16), pl.BlockSpec((MLA_HEADS, tm, QW), lambda i: (0, i, 0))),
        (jax.ShapeDtypeStruct((n, QW), BF16), row(QW)),
        (jax.ShapeDtypeStruct((depth, n, MLA_KV_LORA), F32),
         pl.BlockSpec((1, tm, MLA_KV_LORA), lambda i: (layer, i, 0))),
        (jax.ShapeDtypeStruct((depth, n, MLA_ROPE), F32), pl.BlockSpec((1, tm, MLA_ROPE), lambda i: (layer, i, 0))),
        (jax.ShapeDtypeStruct((n, 512), F32), row(512)),
        (jax.ShapeDtypeStruct((n, 256), BF16), row(256)),
        (jax.ShapeDtypeStruct((n, 256), BF16), row(256)),
        (jax.ShapeDtypeStruct((n, LANES), F32), row(LANES)),
        (jax.ShapeDtypeStruct((n, 256), F32), row(256)),
        (jax.ShapeDtypeStruct((n, 768), BF16), row(768)),
    ]
    in_specs = [row(d), _const_spec((1, d)), _const_spec(w_perm.shape), _const_spec((1, MLA_Q_LORA)),
                _const_spec((1, MLA_KV_LORA)), _const_spec(wq.shape), tab(QW), tab(LANES), tab(LANES)]
    args = [x, g, w_perm, q_norm, kv_norm, wq, tabq, cos_t, sin_t]
    aliases = {}
    if stacks is not None:
        aliases = {len(args): 2, len(args) + 1: 3}
        in_specs += [pl.BlockSpec(memory_space=pl.ANY)] * 2
        args += list(stacks)
    return pl.pallas_call(
        _inproj_body,
        grid=(n // tm,),
        in_specs=in_specs,
        out_specs=[o[1] for o in outs],
        out_shape=[o[0] for o in outs],
        scratch_shapes=[pltpu.VMEM((tm, d), BF16)],
        input_output_aliases=aliases,
        compiler_params=_params(("parallel",)),
        name="in_proj",
    )(*args)


def _mla_prompt_body(q_ref, k_ref, o_ref, acc_ref, s_ref, *, tq, tk, nsub):
    i = pl.program_id(1)

    def keys(j):
        return k_ref[0, pl.ds(pl.multiple_of(j * tk, tk), tk), :]

    def consume(k, m_prev, nxt, shift):
        m_out = []
        if shift is not None:
            r = lax.broadcasted_iota(jnp.int32, (tq, tk), 0)
            c = lax.broadcasted_iota(jnp.int32, (tq, tk), 1)
            visible = (c // CHUNK) - (r // CHUNK) <= shift
        for h in range(MLA_HEADS):
            s = s_ref[h]
            if shift is not None:
                s = jnp.where(visible, s, NEG_BIG)
            m_new = jnp.maximum(m_prev[h], jnp.max(s, axis=-1, keepdims=True))
            alpha = jnp.exp2(m_prev[h] - m_new)
            p = jnp.exp2(s - m_new).astype(BF16)
            acc_ref[h] = alpha * acc_ref[h] + _dot(p, k)
            if nxt is not None:
                s_ref[h] = _dot_nt(q_ref[h, nxt[0] * tq:(nxt[0] + 1) * tq, :], nxt[1])
            m_out.append(m_new)
        return tuple(m_out)

    acc_ref[...] = jnp.zeros_like(acc_ref)
    k0 = keys(0)
    for h in range(MLA_HEADS):
        s_ref[h] = _dot_nt(q_ref[h, 0:tq, :], k0)
    m0 = tuple(jnp.full((tq, 1), NEG_BIG, F32) for _ in range(MLA_HEADS))
    for u in range(nsub):
        start = (i * nsub + u) * tq
        j_last = start // tk
        m1 = lax.fori_loop(0, j_last, lambda j, m: consume(keys(j), m, (u, keys(j + 1)), None), m0)
        consume(keys(j_last), m1, (u + 1, keys(0)) if u + 1 < nsub else None, (start - j_last * tk) // CHUNK)
        for h in range(MLA_HEADS):
            acc = acc_ref[h]
            o = acc[:, :MLA_KV_LORA] * (1.0 / acc[:, _K_ONE:_K_ONE + 1])
            o_ref[u * tq:(u + 1) * tq, h * MLA_KV_LORA:(h + 1) * MLA_KV_LORA] = o.astype(BF16)


def _mla_prompt(q, kmat, batch, seq):
    n = batch * seq
    tq = 256 if seq % 256 == 0 else seq
    tk = 2 * tq if seq % (2 * tq) == 0 else tq
    assert tq % CHUNK == 0 or tq == seq
    nsub = next(c for c in (8, 4, 2, 1) if seq % (c * tq) == 0)
    nq = seq // (nsub * tq)
    tqs = nsub * tq
    return pl.pallas_call(
        functools.partial(_mla_prompt_body, tq=tq, tk=tk, nsub=nsub),
        grid=(batch, nq),
        in_specs=[pl.BlockSpec((MLA_HEADS, tqs, QW), lambda b, i: (0, b * nq + i, 0)),
                  pl.BlockSpec((1, seq, QW), lambda b, i: (b, 0, 0))],
        out_specs=pl.BlockSpec((tqs, MLA_HEADS * MLA_KV_LORA), lambda b, i: (b * nq + i, 0)),
        out_shape=jax.ShapeDtypeStruct((n, MLA_HEADS * MLA_KV_LORA), BF16),
        scratch_shapes=[pltpu.VMEM((MLA_HEADS, tq, QW), F32), pltpu.VMEM((MLA_HEADS, tq, tk), F32)],
        compiler_params=_params(("parallel", "arbitrary")),
        name="mla_prompt",
    )(q, kmat.reshape(batch, seq, QW))


def _mla_sample_body(*refs, tq, has_bias):
    if has_bias:
        q_ref, ckvc_ref, kpec_ref, kn_ref, bc_ref, bn_ref, o_ref = refs
    else:
        q_ref, ckvc_ref, kpec_ref, kn_ref, o_ref = refs
    rows = MLA_HEADS * tq
    q = q_ref[...].reshape(rows, QW)
    ckv_c = ckvc_ref[0].astype(BF16)
    kpe_c = kpec_ref[0].astype(BF16)
    kn = kn_ref[...]
    r0, r1 = MLA_KV_LORA, MLA_KV_LORA + MLA_ROPE
    sc = _dot_nt(q[:, :r0], ckv_c) + _dot_nt(q[:, r0:r1], kpe_c) + _dot_nt(q[:, r1:r1 + MLA_ROPE], kpe_c)
    sn = _dot_nt(q, kn)
    if has_bias:
        sc = sc + jnp.concatenate([bc_ref[...]] * MLA_HEADS, axis=0)
        sn = sn + jnp.concatenate([bn_ref[...]] * MLA_HEADS, axis=0)
    m = jnp.maximum(jnp.max(sc, axis=-1, keepdims=True), jnp.max(sn, axis=-1, keepdims=True))
    pc = jnp.exp2(sc - m)
    pn = jnp.exp2(sn - m)
    l = jnp.sum(pc, axis=-1, keepdims=True) + jnp.sum(pn, axis=-1, keepdims=True)
    o = (_dot(pc.astype(BF16), ckv_c) + _dot(pn.astype(BF16), kn[:, :MLA_KV_LORA])) / l
    for h in range(MLA_HEADS):
        o_ref[:, h * MLA_KV_LORA:(h + 1) * MLA_KV_LORA] = o[h * tq:(h + 1) * tq].astype(BF16)


def _mla_sample(q, kmat_new, cache_ckv, cache_kpe, batch, seq):
    n = batch * seq
    past_len = cache_ckv.shape[1]
    q_pos = past_len + np.arange(seq)
    k_pos = np.arange(past_len + seq)
    mask = (k_pos[None, :] // CHUNK) <= (q_pos[:, None] // CHUNK)
    has_bias = not bool(mask.all())
    in_specs = [pl.BlockSpec((MLA_HEADS, seq, QW), lambda b: (0, b, 0)),
                pl.BlockSpec((1, past_len, MLA_KV_LORA), lambda b: (b, 0, 0)),
                pl.BlockSpec((1, past_len, MLA_ROPE), lambda b: (b, 0, 0)),
                pl.BlockSpec((seq, QW), lambda b: (b, 0))]
    args = [q, cache_ckv, cache_kpe, kmat_new]
    if has_bias:
        bias = np.where(mask, 0.0, NEG_BIG).astype(np.float32)
        in_specs += [_const_spec((seq, past_len)), _const_spec((seq, seq))]
        args += [jnp.asarray(bias[:, :past_len]), jnp.asarray(bias[:, past_len:])]
    return pl.pallas_call(
        functools.partial(_mla_sample_body, tq=seq, has_bias=has_bias),
        grid=(batch,),
        in_specs=in_specs,
        out_specs=pl.BlockSpec((seq, MLA_HEADS * MLA_KV_LORA), lambda b: (b, 0)),
        out_shape=jax.ShapeDtypeStruct((n, MLA_HEADS * MLA_KV_LORA), BF16),
        compiler_params=_params(("parallel",)),
        name="mla_sample",
    )(*args)


def _mlstm_body(qk_ref, v_ref, og_ref, gates_ref, cw_ref, cb_ref, gb_ref, c0_ref, n0_ref, m0_ref, conv0_ref,
                out_ref, c1_ref, n1_ref, m1_ref, conv1_ref,
                xbuf, qk_s, g_s, cbd, n_s, m_s, *, tb, chunk):
    j = pl.program_id(1)
    H, DK, DV = ML_HEADS, ML_DK, ML_DV
    HD = H * DK
    L = chunk
    pad = 8

    @pl.when(j == 0)
    def _init():
        xbuf[0:pad, :] = conv0_ref[0]
        cbd[...] = jnp.zeros_like(cbd)
        for h in range(H):
            cbd[h * DK:(h + 1) * DK, h * DV:(h + 1) * DV] = c0_ref[0, h]
        n_s[...] = n0_ref[0]
        m_s[...] = m0_ref[0]

    _mlstm_conv(qk_ref, cw_ref, cb_ref, xbuf, qk_s, tb, pad)
    gl = gates_ref[...] + gb_ref[...]
    lane_g = lax.broadcasted_iota(jnp.int32, (1, LANES), 1)
    g_s[...] = jnp.where(lane_g < H, gl, jnp.minimum(gl, 0.0) - jnp.log1p(jnp.exp(-jnp.abs(gl))))

    ri = lax.broadcasted_iota(jnp.int32, (L, L), 0)
    ci = lax.broadcasted_iota(jnp.int32, (L, L), 1)
    tri = (ci <= ri).astype(BF16)
    sel = (lax.broadcasted_iota(jnp.int32, (8, LANES), 0) == lax.broadcasted_iota(jnp.int32, (8, LANES), 1)).astype(BF16)
    row_l = lax.broadcasted_iota(jnp.int32, (L, H * L), 0)
    col_l = lax.broadcasted_iota(jnp.int32, (L, H * L), 1)
    causal = (col_l % L) <= row_l
    grp_d = _lane_group(HD, DK)
    rgrp = lax.broadcasted_iota(jnp.int32, (HD, 1), 0) // DK
    bd_mask = rgrp == _lane_group(H * DV, DV)

    def chunk_step(c, carry):
        r0 = pl.multiple_of(c * L, L)
        q = qk_s[pl.ds(r0, L), 0:HD]
        k = qk_s[pl.ds(r0, L), HD:2 * HD]
        v = v_ref[pl.ds(r0, L), :]
        g = g_s[pl.ds(r0, L), :]
        cs = _sel_dot(tri, g)
        x = jnp.where(lane_g < H, g, cs)
        xt = _sel_dot_nt(sel, x)
        li_c = [x[:, h:h + 1] for h in range(H)]
        b_c = [x[:, H + h:H + h + 1] for h in range(H)]
        m_prev = [m_s[:, h:h + 1] for h in range(H)]
        row_e = jnp.concatenate([xt[h:h + 1, :] - xt[H + h:H + h + 1, :] for h in range(H)], axis=1)
        logw = jnp.where(causal, _expand_cols(b_c, L) + row_e, -jnp.inf)
        m_intra = _seg_max(logw, L, H)
        log_inter = [b_c[h] + m_prev[h] for h in range(H)]
        m_t = [jnp.maximum(log_inter[h], m_intra[h]) for h in range(H)]
        w = jnp.exp(logw - _expand_cols(m_t, L))
        qb = q.astype(BF16)
        kexp = jnp.concatenate([jnp.where(grp_d == h, k, 0.0) for h in range(H)], axis=0).astype(BF16)
        vexp = jnp.concatenate([jnp.where(grp_d == h, v, 0.0) for h in range(H)], axis=0).astype(BF16)
        p = _dot_nt(qb, kexp) * w
        den_intra = _seg_sum(p, L, H)
        num = _dot(p.astype(BF16), vexp)
        w_inter = [jnp.exp(log_inter[h] - m_t[h]) for h in range(H)]
        num = num + _expand_cols(w_inter, DV) * _dot(qb, cbd[...].astype(BF16))
        qn = _seg_sum(q * n_s[...], DK, H)
        inv = [1.0 / jnp.maximum(jnp.abs(den_intra[h] + w_inter[h] * qn[h]), jnp.exp(-m_t[h])) for h in range(H)]
        hout = num * _expand_cols(inv, DV)
        og = og_ref[pl.ds(r0, L), :].astype(F32)
        out_ref[pl.ds(r0, L), :] = (jax.nn.sigmoid(og) * hout).astype(BF16)
        m_new = [m_t[h][L - 1:L, :] for h in range(H)]
        b_last = [b_c[h][L - 1:L, :] for h in range(H)]
        w_s = [jnp.exp(b_last[h] - b_c[h] + li_c[h] - m_new[h]) for h in range(H)]
        decay = [jnp.exp(b_last[h] + m_prev[h] - m_new[h]) for h in range(H)]
        kw = k * _expand_cols(w_s, DK)
        upd = _dot_tn(kw.astype(BF16), v.astype(BF16))
        dcol = decay[H - 1]
        for h in range(H - 2, -1, -1):
            dcol = jnp.where(rgrp == h, decay[h], dcol)
        cbd[...] = dcol * cbd[...] + jnp.where(bd_mask, upd, 0.0)
        n_s[...] = _expand_cols(decay, DK) * n_s[...] + jnp.sum(kw, axis=0, keepdims=True)
        m_row = m_s[...]
        for h in range(H):
            m_row = jnp.where(lane_g == h, m_new[h], m_row)
        m_s[...] = m_row
        return carry

    lax.fori_loop(0, tb // L, chunk_step, 0)

    @pl.when(j == pl.num_programs(1) - 1)
    def _fin():
        for h in range(H):
            c1_ref[0, h] = cbd[h * DK:(h + 1) * DK, h * DV:(h + 1) * DV]
        n1_ref[0] = n_s[...]
        m1_ref[0] = m_s[...]
        conv1_ref[0] = xbuf[0:pad, :]


def _mlstm_conv(qk_ref, cw_ref, cb_ref, xbuf, qk_s, tb, pad):
    HD = ML_HEADS * ML_DK
    xbuf[pad:pad + tb, :] = qk_ref[...]
    conv = cb_ref[...] + cw_ref[ML_CONV - 1:ML_CONV, :] * xbuf[pad:pad + tb, :]
    for t in range(1, ML_CONV):
        conv = conv + cw_ref[ML_CONV - 1 - t:ML_CONV - t, :] * xbuf[pad - t:pad - t + tb, :]
    act = conv * jax.nn.sigmoid(conv)
    lane = lax.broadcasted_iota(jnp.int32, (1, 2 * HD), 1)
    qk_s[...] = jnp.where(lane < HD, act, act * (ML_DK ** -0.5))
    xbuf[0:pad, :] = xbuf[tb:tb + pad, :]


def _mlstm_fast_body(qk_ref, v_ref, og_ref, gates_ref, cw_ref, cb_ref, gbc_ref, utri_ref, ecols_ref,
                     c0_ref, n0_ref, mc0_ref, mr0_ref, conv0_ref,
                     out_ref, c1_ref, n1_ref, m1_ref, conv1_ref,
                     xbuf, qk_s, y_s, u_s, cb_s, ks_s, dec_s, nr_s, num_s, den_s, wi_s, cbd, n_s, m_c, mrow_s, *, tb):
    j = pl.program_id(1)
    H, DK, DV = ML_HEADS, ML_DK, ML_DV
    HD = H * DK
    L = DK
    nc = tb // L
    pad = 8

    @pl.when(j == 0)
    def _init():
        xbuf[0:pad, :] = conv0_ref[0]
        cbd[...] = jnp.zeros_like(cbd)
        for h in range(H):
            cbd[h * DK:(h + 1) * DK, h * DV:(h + 1) * DV] = c0_ref[0, h]
        n_s[...] = n0_ref[0]
        m_c[...] = mc0_ref[0]
        mrow_s[...] = mr0_ref[0]

    g8 = gates_ref[...].T[0:8, :] + gbc_ref[...]
    lsig = jnp.minimum(g8, 0.0) - jnp.log1p(jnp.exp(-jnp.abs(g8)))
    lf8 = pltpu.roll(lsig, 8 - H, axis=0)
    fh, fm, fl = _split3(lf8)
    b8 = _dot(fh, utri_ref[...]) + _dot(fm, utri_ref[...]) + _dot(fl, utri_ref[...])
    a8 = g8 - b8
    lane_t = lax.broadcasted_iota(jnp.int32, (1, tb), 1)
    cmax = a8
    sh = 1
    while sh < tb:
        cmax = jnp.maximum(cmax, jnp.where(lane_t >= sh, pltpu.roll(cmax, sh, axis=1), -jnp.inf))
        sh *= 2
    mp8 = jnp.maximum(m_c[:, 0:1], cmax)
    mt8 = b8 + mp8
    valid = lax.broadcasted_iota(jnp.int32, (8, 1), 0) < H
    e8 = jnp.minimum(jnp.exp(-mt8), _F32_BIG)
    x32 = jnp.concatenate([jnp.where(valid, x, 0.0) for x in (mp8, a8, e8, mt8)], axis=0)
    xh, xm, xl = _split3(x32)
    ec = ecols_ref[...]
    y_s[...] = _dot_tn(xh, ec) + _dot_tn(xm, ec) + _dot_tn(xl, ec)
    m_c[...] = jnp.broadcast_to(mt8[:, tb - 1:tb], m_c.shape)

    _mlstm_conv(qk_ref, cw_ref, cb_ref, xbuf, qk_s, tb, pad)

    grp = _lane_group(HD, DK)
    rgrp = lax.broadcasted_iota(jnp.int32, (HD, 1), 0) // DK
    bd_mask = rgrp == grp
    seg_b = bd_mask.astype(BF16)
    row_l = lax.broadcasted_iota(jnp.int32, (L, HD), 0)
    col_l = lax.broadcasted_iota(jnp.int32, (L, HD), 1) % L
    causal = col_l <= row_l
    diag = col_l == row_l

    for c in range(nc):
        r0 = c * L
        q = qk_s[r0:r0 + L, 0:HD]
        k = qk_s[r0:r0 + L, HD:2 * HD]
        vb = v_ref[r0:r0 + L, :].astype(BF16)
        m_e = y_s[r0:r0 + L, 0:HD]
        a_e = y_s[r0:r0 + L, HD:2 * HD]
        m_prev = mrow_s[...] if c == 0 else y_s[r0 - 1:r0, 0:HD]
        m_last = m_e[L - 1:L, :]
        row_a = jnp.sum(jnp.where(diag, a_e, 0.0), axis=0, keepdims=True)
        d = jnp.exp(jnp.where(causal, row_a - m_e, -jnp.inf))
        qb = q.astype(BF16)
        kb = k.astype(BF16)
        kexp = jnp.concatenate([jnp.where(grp == h, kb, jnp.zeros_like(kb)) for h in range(H)], axis=0)
        vexp = jnp.concatenate([jnp.where(grp == h, vb, jnp.zeros_like(vb)) for h in range(H)], axis=0)
        p = _dot_nt(qb, kexp) * d
        p_hi = p.astype(BF16)
        p_lo = (p - p_hi.astype(F32)).astype(BF16)
        num_s[r0:r0 + L, :] = _dot(p_hi, vexp)
        den_s[r0:r0 + L, :] = _dot(p_hi, seg_b) + _dot(p_lo, seg_b)
        wi_s[r0:r0 + L, :] = jnp.exp(m_prev - m_e)
        kw = k * jnp.exp(a_e - m_last)
        u_s[c] = jnp.where(bd_mask, _dot_tn(kw.astype(BF16), vb), 0.0)
        ks_s[c:c + 1, :] = jnp.sum(kw, axis=0, keepdims=True)
        dec_s[c:c + 1, :] = jnp.exp(m_prev - m_last)

    for c in range(nc):
        cb_s[c] = cbd[...].astype(BF16)
        nr_s[c:c + 1, :] = n_s[...]
        dec = dec_s[c:c + 1, :]
        cbd[...] = dec * cbd[...] + u_s[c]
        n_s[...] = dec * n_s[...] + ks_s[c:c + 1, :]

    for c in range(nc):
        r0 = c * L
        q = qk_s[r0:r0 + L, 0:HD]
        qn = q * nr_s[c:c + 1, :]
        qn_hi = qn.astype(BF16)
        qn_lo = (qn - qn_hi.astype(F32)).astype(BF16)
        w_inter = wi_s[r0:r0 + L, :]
        num = num_s[r0:r0 + L, :] + w_inter * _dot(q.astype(BF16), cb_s[c])
        den = den_s[r0:r0 + L, :] + w_inter * (_dot(qn_hi, seg_b) + _dot(qn_lo, seg_b))
        hout = num / jnp.maximum(jnp.abs(den), y_s[r0:r0 + L, 2 * HD:3 * HD])
        og = og_ref[r0:r0 + L, :].astype(F32)
        out_ref[r0:r0 + L, :] = (jax.nn.sigmoid(og) * hout).astype(BF16)

    mrow_s[...] = y_s[tb - 1:tb, 3 * HD:4 * HD]

    @pl.when(j == pl.num_programs(1) - 1)
    def _fin():
        for h in range(H):
            c1_ref[0, h] = cbd[h * DK:(h + 1) * DK, h * DV:(h + 1) * DV]
        n1_ref[0] = n_s[...]
        m1_ref[0] = m_c[...]
        conv1_ref[0] = xbuf[0:pad, :]


def _mlstm_fast(qk, v, og, gates, conv_w, conv_b, gate_bias, c0, n0, m0, conv0, batch, seq, tb):
    n = batch * seq
    H, DK, DV = ML_HEADS, ML_DK, ML_DV
    HD = H * DK
    nb = seq // tb
    nc = tb // DK
    pad = 8
    n0p = n0.reshape(batch, 1, HD)
    mc0 = jnp.broadcast_to(jnp.pad(m0, ((0, 0), (0, 8 - H)))[:, :, None], (batch, 8, LANES))
    mr0 = jnp.repeat(m0, DV, axis=1).reshape(batch, 1, HD)
    conv0p = jnp.pad(conv0, ((0, 0), (pad - (ML_CONV - 1), 0), (0, 0)))
    gbc = jnp.concatenate([gate_bias[0], gate_bias[1]])[:, None]
    idx = np.arange(tb)
    utri = jnp.asarray(idx[:, None] <= idx[None, :], BF16)
    ecols_np = np.zeros((32, 4 * HD), np.float32)
    for kq in range(4):
        for h in range(H):
            ecols_np[8 * kq + h, kq * HD + h * DV:kq * HD + (h + 1) * DV] = 1.0
    ecols = jnp.asarray(ecols_np, BF16)
    row = lambda w: pl.BlockSpec((tb, w), lambda b, j: (b * nb + j, 0))
    per_b = lambda shape: pl.BlockSpec((1,) + shape, lambda b, j: (b,) + (0,) * len(shape))
    f32 = lambda *shape: pltpu.VMEM(shape, F32)
    out, c1, n1, m1, conv1 = pl.pallas_call(
        functools.partial(_mlstm_fast_body, tb=tb),
        grid=(batch, nb),
        in_specs=[row(2 * HD), row(H * DV), row(H * DV), row(LANES),
                  _const_spec((ML_CONV, 2 * HD)), _const_spec((1, 2 * HD)), _const_spec((2 * H, 1)),
                  _const_spec((tb, tb)), _const_spec((32, 4 * HD)),
                  per_b((H, DK, DV)), per_b((1, HD)), per_b((8, LANES)), per_b((1, HD)), per_b((pad, 2 * HD))],
        out_specs=[row(H * DV), per_b((H, DK, DV)), per_b((1, HD)), per_b((8, LANES)), per_b((pad, 2 * HD))],
        out_shape=[jax.ShapeDtypeStruct((n, H * DV), BF16), jax.ShapeDtypeStruct((batch, H, DK, DV), F32),
                   jax.ShapeDtypeStruct((batch, 1, HD), F32), jax.ShapeDtypeStruct((batch, 8, LANES), F32),
                   jax.ShapeDtypeStruct((batch, pad, 2 * HD), F32)],
        scratch_shapes=[f32(tb + pad, 2 * HD), f32(tb, 2 * HD), f32(tb, 4 * HD), f32(nc, HD, H * DV),
                        pltpu.VMEM((nc, HD, H * DV), BF16), f32(nc, HD), f32(nc, HD), f32(nc, HD),
                        f32(tb, HD), f32(tb, HD), f32(tb, HD), f32(HD, H * DV), f32(1, HD), f32(8, LANES),
                        f32(1, HD)],
        compiler_params=_params(("parallel", "arbitrary")),
        name="mlstm_blocked",
    )(qk, v, og, gates, conv_w, conv_b, gbc, utri, ecols, c0, n0p, mc0, mr0, conv0p)
    return (out, c1, n1.reshape(batch, H, DK), m1[:, :H, 0], conv1[:, pad - (ML_CONV - 1):, :])


def _mlstm(qk, v, og, gates, conv_w, conv_b, gate_bias, c0, n0, m0, conv0, batch, seq):
    if ML_DK == ML_DV == CHUNK and seq % 512 == 0:
        return _mlstm_fast(qk, v, og, gates, conv_w, conv_b, gate_bias, c0, n0, m0, conv0, batch, seq, 512)
    gb = gate_bias
    gbias = jnp.concatenate([gb[0], gb[1], jnp.zeros((LANES - 2 * ML_HEADS,), F32)])[None, :]
    n = batch * seq
    H, DK, DV = ML_HEADS, ML_DK, ML_DV
    chunk = min(CHUNK, seq)
    assert seq % chunk == 0 and seq >= ML_CONV - 1 and chunk % 8 == 0
    tb = 512 if seq % 512 == 0 else chunk
    nb = seq // tb
    pad = 8
    n0p = n0.reshape(batch, 1, H * DK)
    m0p = jnp.pad(m0.reshape(batch, 1, H), ((0, 0), (0, 0), (0, LANES - H)))
    conv0p = jnp.pad(conv0, ((0, 0), (pad - (ML_CONV - 1), 0), (0, 0)))
    row = lambda w: pl.BlockSpec((tb, w), lambda b, j: (b * nb + j, 0))
    per_b = lambda shape: pl.BlockSpec((1,) + shape, lambda b, j: (b,) + (0,) * len(shape))
    out, c1, n1, m1, conv1 = pl.pallas_call(
        functools.partial(_mlstm_body, tb=tb, chunk=chunk),
        grid=(batch, nb),
        in_specs=[row(2 * H * DK), row(H * DV), row(H * DV), row(LANES),
                  _const_spec((ML_CONV, 2 * H * DK)), _const_spec((1, 2 * H * DK)), _const_spec((1, LANES)),
                  per_b((H, DK, DV)), per_b((1, H * DK)), per_b((1, LANES)), per_b((pad, 2 * H * DK))],
        out_specs=[row(H * DV), per_b((H, DK, DV)), per_b((1, H * DK)), per_b((1, LANES)), per_b((pad, 2 * H * DK))],
        out_shape=[jax.ShapeDtypeStruct((n, H * DV), BF16), jax.ShapeDtypeStruct((batch, H, DK, DV), F32),
                   jax.ShapeDtypeStruct((batch, 1, H * DK), F32), jax.ShapeDtypeStruct((batch, 1, LANES), F32),
                   jax.ShapeDtypeStruct((batch, pad, 2 * H * DK), F32)],
        scratch_shapes=[pltpu.VMEM((tb + pad, 2 * H * DK), F32), pltpu.VMEM((tb, 2 * H * DK), F32),
                        pltpu.VMEM((tb, LANES), F32), pltpu.VMEM((H * DK, H * DV), F32),
                        pltpu.VMEM((1, H * DK), F32), pltpu.VMEM((1, LANES), F32)],
        compiler_params=_params(("parallel", "arbitrary")),
        name="mlstm",
    )(qk, v, og, gates, conv_w, conv_b, gbias, c0, n0p, m0p, conv0p)
    return (out, c1, n1.reshape(batch, H, DK), m1[:, 0, :H], conv1[:, pad - (ML_CONV - 1):, :])


def _hgrn_body(hgf_ref, hgr_ref, lbraw_ref, hnorm_ref, tri_ref, s0_ref, out_ref, s1_ref,
               hq_s, kk_s, f_s, b_s, qe_s, oi_s, u_s, sts_s, dec_s, st, *, tb, sub, layer):
    j = pl.program_id(1)
    H, DK, DV = HG_HEADS, HG_DK, HG_DV
    HD = H * DK
    S = sub

    @pl.when(j == 0)
    def _init():
        st[...] = jnp.zeros_like(st)
        for h in range(H):
            st[h * DV:(h + 1) * DV, h * DK:(h + 1) * DK] = s0_ref[0, h].T

    raw = lbraw_ref[...]
    e = jnp.exp(raw - jnp.max(raw, axis=0, keepdims=True))
    sm = e / jnp.sum(e, axis=0, keepdims=True)
    cum = sm[0:1, :]
    for l in range(1, layer + 1):
        cum = cum + sm[l:l + 1, :]
    lb = cum - sm[0:1, :]

    z = hgf_ref[...]
    lsig = jnp.minimum(z, 0.0) - jnp.log1p(jnp.exp(-jnp.abs(z)))
    a1 = jnp.log(lb)
    a2 = jnp.log1p(-lb) + lsig
    delta = a1 - a2
    lf = jnp.where(jnp.isnan(delta), a1 + a2, jnp.maximum(a1, a2) + jnp.log1p(jnp.exp(-jnp.abs(delta))))
    f_s[...] = jnp.exp(lf)
    b_all = _sel_dot(tri_ref[...], lf)
    b_s[...] = b_all
    kk_s[...] = (1.0 - lb) * jax.nn.sigmoid(-z)
    xq = hgr_ref[:, 0:HD].astype(F32)
    hq_all = xq * jax.nn.sigmoid(xq)
    hq_s[...] = hq_all
    qe_s[...] = (hq_all * jnp.exp(b_all)).astype(BF16)

    rowi = lax.broadcasted_iota(jnp.int32, (S, 1), 0)
    seg = ((lax.broadcasted_iota(jnp.int32, (HD, 1), 0) // DK) == _lane_group(H * DV, DV))
    seg_b = seg.astype(BF16)
    gnorm = hnorm_ref[...]

    n_sub = tb // S

    for c in range(n_sub):
        r0 = c * S
        hq = hq_s[r0:r0 + S, :]
        kk = kk_s[r0:r0 + S, :]
        f = f_s[r0:r0 + S, :]
        b = b_s[r0:r0 + S, :]
        iv_b = hgr_ref[r0:r0 + S, HD:HD + H * DV]
        iv = iv_b.astype(F32)
        bl = b[S - 1:S, :]
        kd = kk * jnp.exp(bl - b)
        u_s[c] = jnp.where(seg, _dot_tn(iv_b, kd.astype(BF16)), 0.0)
        dec_s[c:c + 1, :] = jnp.exp(bl)
        qd = jnp.where(rowi == S - 1, hq, 0.0)
        ws = [None] * S
        ws[S - 1] = (qd * kk[S - 1:S, :]).astype(BF16)
        for s in range(S - 2, -1, -1):
            qd = jnp.where(rowi == s, hq, qd * f[s + 1:s + 2, :])
            ws[s] = (qd * kk[s:s + 1, :]).astype(BF16)
        r = _dot(jnp.concatenate(ws, axis=0), seg_b)
        oi = r[0:S, :] * iv[0:1, :]
        for s in range(1, S):
            oi = oi + r[s * S:(s + 1) * S, :] * iv[s:s + 1, :]
        oi_s[r0:r0 + S, :] = oi

    for c in range(n_sub):
        sts_s[c] = st[...].astype(BF16)
        st[...] = dec_s[c:c + 1, :] * st[...] + u_s[c]

    grp_rows = min(n_sub, 4) * S
    for c0 in range(0, n_sub, grp_rows // S):
        r0 = c0 * S
        o = jnp.concatenate([oi_s[(c0 + i) * S:(c0 + i + 1) * S, :]
                             + _dot_nt(qe_s[(c0 + i) * S:(c0 + i + 1) * S, :], sts_s[c0 + i])
                             for i in range(grp_rows // S)], axis=0)
        gt = hgr_ref[r0:r0 + grp_rows, HD + H * DV:HD + 2 * H * DV].astype(F32)
        ms = _dot((o * o).astype(BF16), seg_b) * (1.0 / DV)
        y = o * lax.rsqrt(ms + EPS) * gnorm * (gt * jax.nn.sigmoid(gt))
        out_ref[r0:r0 + grp_rows, :] = y.astype(BF16)

    @pl.when(j == pl.num_programs(1) - 1)
    def _fin():
        for h in range(H):
            s1_ref[0, h] = st[h * DV:(h + 1) * DV, h * DK:(h + 1) * DK].T


def _hgrn(hg_f, hg_r, lb_raw, hnorm, s0, batch, seq, layer):
    n = batch * seq
    H, DK, DV = HG_HEADS, HG_DK, HG_DV
    assert DK == DV
    sub = min(HG_SUB, seq)
    assert seq % sub == 0 and sub % 8 == 0
    tb = 512 if seq % 512 == 0 else sub
    nb = seq // tb
    depth = lb_raw.shape[0]
    n_sub = tb // sub
    idx = np.arange(tb)
    tri = jnp.asarray((idx[:, None] // sub == idx[None, :] // sub) & (idx[None, :] <= idx[:, None]), BF16)
    per_b = pl.BlockSpec((1, H, DK, DV), lambda b, j: (b, 0, 0, 0))
    out, s1 = pl.pallas_call(
        functools.partial(_hgrn_body, tb=tb, sub=sub, layer=layer),
        grid=(batch, nb),
        in_specs=[pl.BlockSpec((tb, H * DK), lambda b, j: (b * nb + j, 0)),
                  pl.BlockSpec((tb, H * DK + 2 * H * DV), lambda b, j: (b * nb + j, 0)),
                  _const_spec((depth, H * DK)), _const_spec((1, H * DV)), _const_spec((tb, tb)), per_b],
        out_specs=[pl.BlockSpec((tb, H * DV), lambda b, j: (b * nb + j, 0)), per_b],
        out_shape=[jax.ShapeDtypeStruct((n, H * DV), BF16), jax.ShapeDtypeStruct((batch, H, DK, DV), F32)],
        scratch_shapes=[pltpu.VMEM((tb, H * DK), F32), pltpu.VMEM((tb, H * DK), F32), pltpu.VMEM((tb, H * DK), F32),
                        pltpu.VMEM((tb, H * DK), F32), pltpu.VMEM((tb, H * DK), BF16),
                        pltpu.VMEM((tb, H * DV), F32), pltpu.VMEM((n_sub, H * DV, H * DK), F32),
                        pltpu.VMEM((n_sub, H * DV, H * DK), BF16), pltpu.VMEM((max(n_sub, 8), H * DK), F32),
                        pltpu.VMEM((H * DV, H * DK), F32)],
        compiler_params=_params(("parallel", "arbitrary")),
        name="hgrn2",
    )(hg_f, hg_r, lb_raw, hnorm, tri, s0)
    return out, s1


def _outproj_ffn_body(x_ref, o_ref, ml_ref, hgo_ref, g_ref, w1_ref, w2_ref, w3_ref,
                      gpre_ref, gpost_ref, wg_ref, wu_ref, wo_ref, y_ref, xn_ref, acc_ref, *, fc):
    mix = _dot(o_ref[...], w1_ref[...]) + _dot(ml_ref[...], w2_ref[...]) + _dot(hgo_ref[...], w3_ref[...])
    x1 = x_ref[...] + _rms(mix, g_ref[...])
    y_ref[...] = _ffn_apply(x1, gpre_ref, gpost_ref, wg_ref, wu_ref, wo_ref, xn_ref, acc_ref, fc)


def _outproj_ffn(x, o_lat, ml_out, hg_out, g, w1, w2, w3, g_pre, g_post, w_gate, w_up, w_out):
    n, d = x.shape
    d_ff = w_gate.shape[1]
    tm = _token_tile(n)
    fc = 256 if d_ff % 256 == 0 else d_ff
    row = lambda w: pl.BlockSpec((tm, w), lambda i: (i, 0))
    return pl.pallas_call(
        functools.partial(_outproj_ffn_body, fc=fc),
        grid=(n // tm,),
        in_specs=[row(d), row(o_lat.shape[1]), row(ml_out.shape[1]), row(hg_out.shape[1]), _const_spec((1, d)),
                  _const_spec(w1.shape), _const_spec(w2.shape), _const_spec(w3.shape),
                  _const_spec((1, d)), _const_spec((1, d)), _const_spec((d, d_ff)), _const_spec((d, d_ff)),
                  _const_spec((d_ff, d))],
        out_specs=row(d),
        out_shape=jax.ShapeDtypeStruct((n, d), F32),
        scratch_shapes=[pltpu.VMEM((tm, d), BF16), pltpu.VMEM((tm, d), F32)],
        compiler_params=_params(("parallel",)),
        name="out_proj_ffn",
    )(x, o_lat, ml_out, hg_out, g, w1, w2, w3, g_pre, g_post, w_gate, w_up, w_out)


def _rope_tables(pos):
    inv = ROPE_THETA ** (-jnp.arange(0, MLA_ROPE, 2, dtype=F32) / MLA_ROPE)
    ang = pos.astype(F32)[:, None] * inv[None, :]
    cos, sin = jnp.cos(ang), jnp.sin(ang)
    t = pos.shape[0]
    cos2, sin2 = jnp.concatenate([cos, cos], -1), jnp.concatenate([sin, sin], -1)
    tabq = (MLA_SCALE * LOG2E) * jnp.concatenate([jnp.ones((t, MLA_KV_LORA), F32), cos2, sin2,
                                        jnp.zeros((t, QW - MLA_KV_LORA - 2 * MLA_ROPE), F32)], -1)
    zpad = jnp.zeros((t, LANES - 2 * MLA_ROPE), F32)
    cos_t = jnp.concatenate([cos2, cos2, zpad], -1)
    sin_t = jnp.concatenate([sin2, sin2, zpad], -1)
    return tabq, cos_t, sin_t


def _swap_halves(w):
    half = w.shape[-1] // 2
    return jnp.concatenate([-w[..., half:], w[..., :half]], axis=-1)


def _layer_weights(l, ln_gains, w_ffn_in, w_ffn_out, w_in, w_out, mla_q_norm, mla_kv_norm, mla_w_uq, mla_w_uk,
                   mla_w_uv, ml_conv_w, ml_conv_b, ml_gate_bias, hg_norm):
    d = w_in.shape[1]
    d_ff = w_ffn_out.shape[2]
    sizes = (MLA_Q_LORA, MLA_KV_LORA, MLA_ROPE, 2 * ML_HEADS * ML_DK, ML_HEADS * ML_DV, ML_HEADS * ML_DV, ML_HEADS,
             ML_HEADS, HG_HEADS * HG_DK, HG_HEADS * HG_DK, HG_HEADS * HG_DV, HG_HEADS * HG_DV)
    pts = np.cumsum((0,) + sizes)
    part = [w_in[l][:, pts[i]:pts[i + 1]] for i in range(len(sizes))]
    zeros = lambda n: jnp.zeros((d, n), F32)
    kpe, kpe_sw = part[2], _swap_halves(part[2])
    w_perm = jnp.concatenate(
        [part[0], part[1], kpe, kpe, zeros(LANES - 2 * MLA_ROPE), kpe_sw, kpe_sw, zeros(LANES - 2 * MLA_ROPE),
         part[3], part[4], part[5], part[6], part[7], zeros(LANES - 2 * ML_HEADS), part[8], part[9], part[10],
         part[11]], axis=1).astype(BF16)
    assert w_perm.shape[1] == _IN_COLS
    uq = mla_w_uq[l].reshape(MLA_Q_LORA, MLA_HEADS, MLA_NOPE + MLA_ROPE)
    uq_nope = jnp.transpose(uq[:, :, :MLA_NOPE], (1, 0, 2))
    uk_t = jnp.transpose(mla_w_uk[l], (1, 2, 0))
    w_lat = _bmm(uq_nope, uk_t)
    uq_pe = jnp.transpose(uq[:, :, MLA_NOPE:], (1, 0, 2))
    wq = jnp.concatenate([w_lat, uq_pe, _swap_halves(uq_pe),
                          jnp.zeros((MLA_HEADS, MLA_Q_LORA, QW - MLA_KV_LORA - 2 * MLA_ROPE), F32)], axis=-1)
    wq = jnp.transpose(wq, (1, 0, 2)).reshape(MLA_Q_LORA, MLA_HEADS * QW).astype(BF16)
    n_mla = MLA_HEADS * MLA_V
    uv = jnp.transpose(mla_w_uv[l], (1, 0, 2))
    wo_mla = w_out[l][:n_mla].reshape(MLA_HEADS, MLA_V, d)
    w1 = _bmm(uv, wo_mla).reshape(MLA_HEADS * MLA_KV_LORA, d).astype(BF16)
    n_ml = ML_HEADS * ML_DV
    w2 = w_out[l][n_mla:n_mla + n_ml].astype(BF16)
    w3 = w_out[l][n_mla + n_ml:].astype(BF16)
    ffn = []
    for j in range(2):
        wi = w_ffn_in[l, j]
        ffn.append((wi[:, :d_ff].astype(BF16), wi[:, d_ff:].astype(BF16), w_ffn_out[l, j].astype(BF16)))
    return dict(ln=ln_gains[l], ffn=ffn, w_perm=w_perm, wq=wq, w1=w1, w2=w2, w3=w3,
                q_norm=mla_q_norm[l][None, :], kv_norm=mla_kv_norm[l][None, :], conv_w=ml_conv_w[l],
                conv_b=ml_conv_b[l][None, :], gate_bias=ml_gate_bias[l].astype(F32), hnorm=jnp.tile(hg_norm[l], HG_HEADS)[None, :])


def _tile_rows(t, reps):
    return jnp.tile(t, (reps, 1)) if reps > 1 else t


def _layer(x, lw, layer, depth, lb_raw, batch, seq, pos, past, stacks):
    n, d = x.shape
    g = lw['ln']
    gain = lambda i: g[i][None, :]
    x = _ffn(x, gain(0), gain(1), *lw['ffn'][0])
    tabq, cos_t, sin_t = _rope_tables(pos)
    tm = _token_tile(n)
    reps = tm // seq if seq < tm else 1
    tabs = [_tile_rows(t, reps) for t in (tabq, cos_t, sin_t)]
    q, kmat, ckv, kpe, mlqk, mlv, mlo, gates, hg_f, hg_r = _inproj(x, gain(2), lw['w_perm'], lw['q_norm'], lw['kv_norm'],
                                                          lw['wq'], *tabs, seq, layer, depth, stacks)
    if past is None:
        o_lat = _mla_prompt(q, kmat, batch, seq)
        c0 = jnp.zeros((batch, ML_HEADS, ML_DK, ML_DV), F32)
        n0 = jnp.zeros((batch, ML_HEADS, ML_DK), F32)
        m0 = jnp.zeros((batch, ML_HEADS), F32)
        conv0 = jnp.zeros((batch, ML_CONV - 1, 2 * ML_HEADS * ML_DK), F32)
        s0 = jnp.zeros((batch, HG_HEADS, HG_DK, HG_DV), F32)
    else:
        cache_ckv, cache_kpe, c0, n0, m0, conv0, s0 = past
        o_lat = _mla_sample(q, kmat, cache_ckv, cache_kpe, batch, seq)
    ml_out, c1, n1, m1, conv1 = _mlstm(mlqk, mlv, mlo, gates, lw['conv_w'], lw['conv_b'], lw['gate_bias'],
                                       c0, n0, m0, conv0, batch, seq)
    hg_out, s1 = _hgrn(hg_f, hg_r, lb_raw, lw['hnorm'], s0, batch, seq, layer)
    x = _outproj_ffn(x, o_lat, ml_out, hg_out, gain(3), lw['w1'], lw['w2'], lw['w3'],
                     gain(4), gain(5), *lw['ffn'][1])
    return x, (ckv, kpe), (c1, n1, m1, conv1, s1)


def kernel(x_prompt, x_sample, cache_ckv, cache_kpe, state_mlstm_c, state_mlstm_n, state_mlstm_m, state_mlstm_conv, state_hgrn, ln_gains, w_ffn_in, w_ffn_out, w_in, w_out, mla_q_norm, mla_kv_norm, mla_w_uq, mla_w_uk, mla_w_uv, ml_conv_w, ml_conv_b, ml_gate_bias, hg_lb_raw, hg_norm):
    depth = w_in.shape[0]
    bp, tp, d = x_prompt.shape
    bs, ts, _ = x_sample.shape
    past_len = cache_ckv.shape[2]
    pos_p = jnp.arange(tp)
    pos_s = past_len + jnp.arange(ts)
    yp = x_prompt.reshape(bp * tp, d)
    ys = x_sample.reshape(bs * ts, d)
    lb_raw = hg_lb_raw.astype(F32)
    p_states, s_states = [], []
    p_kv = s_kv = None
    for l in range(depth):
        lw = _layer_weights(l, ln_gains, w_ffn_in, w_ffn_out, w_in, w_out, mla_q_norm, mla_kv_norm, mla_w_uq,
                            mla_w_uk, mla_w_uv, ml_conv_w, ml_conv_b, ml_gate_bias, hg_norm)
        yp, p_kv, st_p = _layer(yp, lw, l, depth, lb_raw, bp, tp, pos_p, None, p_kv)
        past = (cache_ckv[l], cache_kpe[l], state_mlstm_c[l], state_mlstm_n[l], state_mlstm_m[l],
                state_mlstm_conv[l], state_hgrn[l])
        ys, s_kv, st_s = _layer(ys, lw, l, depth, lb_raw, bs, ts, pos_s, past, s_kv)
        p_states.append(st_p)
        s_states.append(st_s)
    p_out = [jnp.stack([st[i] for st in p_states]) for i in range(5)]
    s_out = [jnp.stack([st[i] for st in s_states]) for i in range(5)]
    return (yp.reshape(bp, tp, d), ys.reshape(bs, ts, d),
            p_kv[0].reshape(depth, bp, tp, MLA_KV_LORA), p_kv[1].reshape(depth, bp, tp, MLA_ROPE), *p_out,
            s_kv[0].reshape(depth, bs, ts, MLA_KV_LORA), s_kv[1].reshape(depth, bs, ts, MLA_ROPE), *s_out)
```
